```python
import functools
import jax, jax.numpy as jnp
from jax import lax
import numpy as np

D_MODEL = 2048
BATCH = 4
SEQ = 4096
DEPTH = 2

N_MIXERS = 2
Q_BLOCK = 128
EPS = 1e-6
MLA_HEADS = 16
MLA_Q_LORA = 512
MLA_KV_LORA = 512
MLA_NOPE = 128
MLA_ROPE = 64
MLA_V = 128
ROPE_THETA = 10000.0
FOX_HEADS = 16
FOX_HEAD_DIM = 128
N_GROUPS = 8
EXPERTS_PER_GROUP = 8
N_EXPERTS = N_GROUPS * EXPERTS_PER_GROUP
TOP_K = 2
D_EXPERT = 512
ROUTE_BLOCK = 128

kernel_name = "hybrid_mla_fox_hier_moe"


def rms_norm(x, gain):
    xf = x.astype(jnp.float32)
    y = xf * lax.rsqrt(jnp.mean(xf * xf, axis=-1, keepdims=True) + EPS)
    return (y * gain.astype(jnp.float32)).astype(x.dtype)


def apply_rope(x, positions):
    half = x.shape[-1] // 2
    freqs = ROPE_THETA ** (-jnp.arange(half, dtype=jnp.float32) / half)
    ang = positions.astype(jnp.float32)[..., None] * freqs
    cos = jnp.cos(ang)[:, :, None, :]
    sin = jnp.sin(ang)[:, :, None, :]
    x1 = x[..., :half].astype(jnp.float32)
    x2 = x[..., half:].astype(jnp.float32)
    out = jnp.concatenate([x1 * cos - x2 * sin, x2 * cos + x1 * sin], axis=-1)
    return out.astype(x.dtype)


def causal_block_attention(q, k, v, scale, log_decay=None):
    B, S, H, Dk = q.shape
    Dv = v.shape[-1]
    nb = S // Q_BLOCK
    q_blocks = q.reshape(B, nb, Q_BLOCK, H, Dk).transpose(1, 0, 2, 3, 4)
    k_pos = jnp.arange(S)
    xs = (jnp.arange(nb), q_blocks)
    if log_decay is not None:
        xs = xs + (log_decay.reshape(B, H, nb, Q_BLOCK).transpose(2, 0, 1, 3),)

    def one_block(blk):
        i, q_i = blk[0], blk[1]
        s = jnp.einsum("bqhd,bkhd->bhqk", q_i, k, preferred_element_type=jnp.float32) * scale
        if log_decay is not None:
            s = s + (blk[2][..., :, None] - log_decay[:, :, None, :])
        q_pos = i * Q_BLOCK + jnp.arange(Q_BLOCK)
        s = jnp.where(q_pos[:, None] >= k_pos[None, :], s, -jnp.inf)
        p = jax.nn.softmax(s, axis=-1)
        return jnp.einsum("bhqk,bkhd->bqhd", p.astype(v.dtype), v)

    out = lax.map(one_block, xs)
    return out.transpose(1, 0, 2, 3, 4).reshape(B, S, H * Dv)


def mla_mixer(h, positions, w_in, q_lat_norm, kv_lat_norm, w_uq, w_ukv, qk_gain, w_o):
    B, S, _ = h.shape
    z = h @ w_in
    c_q = z[..., :MLA_Q_LORA]
    c_kv = z[..., MLA_Q_LORA:MLA_Q_LORA + MLA_KV_LORA]
    k_rope = z[..., MLA_Q_LORA + MLA_KV_LORA:][:, :, None, :]
    q = (rms_norm(c_q, q_lat_norm) @ w_uq).reshape(B, S, MLA_HEADS, MLA_NOPE + MLA_ROPE)
    kv = (rms_norm(c_kv, kv_lat_norm) @ w_ukv).reshape(B, S, MLA_HEADS, MLA_NOPE + MLA_V)
    q_nope = rms_norm(q[..., :MLA_NOPE], qk_gain[0, :MLA_NOPE])
    q_rope = apply_rope(rms_norm(q[..., MLA_NOPE:], qk_gain[0, MLA_NOPE:]), positions)
    k_nope = rms_norm(kv[..., :MLA_NOPE], qk_gain[1, :MLA_NOPE])
    k_rope = apply_rope(rms_norm(k_rope, qk_gain[1, MLA_NOPE:]), positions)
    v = kv[..., MLA_NOPE:]
    q_full = jnp.concatenate([q_nope, q_rope], axis=-1)
    k_full = jnp.concatenate([k_nope, jnp.broadcast_to(k_rope, (B, S, MLA_HEADS, MLA_ROPE))], axis=-1)
    o = causal_block_attention(q_full, k_full, v, 1.0 / math_sqrt(MLA_NOPE + MLA_ROPE))
    return o @ w_o


def math_sqrt(n):
    return float(np.sqrt(n))


def fox_mixer(h, w_in, forget_bias, qk_gain, w_o):
    B, S, _ = h.shape
    HD = FOX_HEADS * FOX_HEAD_DIM
    z = h @ w_in
    q = rms_norm(z[..., :HD].reshape(B, S, FOX_HEADS, FOX_HEAD_DIM), qk_gain[0])
    k = rms_norm(z[..., HD:2 * HD].reshape(B, S, FOX_HEADS, FOX_HEAD_DIM), qk_gain[1])
    v = z[..., 2 * HD:3 * HD].reshape(B, S, FOX_HEADS, FOX_HEAD_DIM)
    f_logit = z[..., 3 * HD:3 * HD + FOX_HEADS]
    gate = z[..., 3 * HD + FOX_HEADS:]
    log_f = jax.nn.log_sigmoid(f_logit.astype(jnp.float32) + forget_bias.astype(jnp.float32))
    c = jnp.cumsum(log_f, axis=1).transpose(0, 2, 1)
    o = causal_block_attention(q, k, v, 1.0 / math_sqrt(FOX_HEAD_DIM), log_decay=c)
    o = o * jax.nn.sigmoid(gate)
    return o @ w_o


def routed_experts(t, e_idx, gate, w_gate_up, w_down):
    T, D = t.shape
    n_assign = T * TOP_K
    e_flat = e_idx.reshape(-1)
    tok_flat = jnp.repeat(jnp.arange(T), TOP_K)
    g_flat = gate.reshape(-1)
    order = jnp.argsort(e_flat)
    e_sorted, tok_sorted, g_sorted = e_flat[order], tok_flat[order], g_flat[order]
    counts = jnp.zeros((N_EXPERTS,), jnp.int32).at[e_flat].add(1)
    start = jnp.cumsum(counts) - counts
    padded = (counts + ROUTE_BLOCK - 1) // ROUTE_BLOCK * ROUTE_BLOCK
    pend = jnp.cumsum(padded)
    pstart = pend - padded
    dest = pstart[e_sorted] + (jnp.arange(n_assign) - start[e_sorted])
    n_blocks = (n_assign + N_EXPERTS * (ROUTE_BLOCK - 1) + ROUTE_BLOCK - 1) // ROUTE_BLOCK
    buf = jnp.zeros((n_blocks * ROUTE_BLOCK, D), t.dtype).at[dest].set(t[tok_sorted])
    block_e = jnp.minimum(
        jnp.searchsorted(pend, jnp.arange(n_blocks) * ROUTE_BLOCK, side="right"), N_EXPERTS - 1)

    def expert_block(blk):
        xb, e = blk
        gu = xb @ w_gate_up[e]
        return (jax.nn.silu(gu[:, :D_EXPERT]) * gu[:, D_EXPERT:]) @ w_down[e]

    y = lax.map(expert_block, (buf.reshape(n_blocks, ROUTE_BLOCK, D), block_e))
    y = y.reshape(n_blocks * ROUTE_BLOCK, D)[dest]
    return jax.ops.segment_sum(y * g_sorted[:, None], tok_sorted, num_segments=T)


def hier_moe(h, w_router_group, b_router_group, w_router_expert, b_router_expert, w_gate_up, w_down):
    B, S, D = h.shape
    T = B * S
    t = h.reshape(T, D)
    g_logits = (t @ w_router_group).astype(jnp.float32) + b_router_group.astype(jnp.float32)
    p_group = jax.nn.softmax(g_logits, axis=-1)
    _, grp = lax.top_k(g_logits, 1)
    p_g = jnp.take_along_axis(p_group, grp, axis=-1)
    e_logits = ((t @ w_router_expert).astype(jnp.float32)
                + b_router_expert.astype(jnp.float32)).reshape(T, N_GROUPS, EXPERTS_PER_GROUP)
    e_in = jnp.take_along_axis(e_logits, grp[:, :, None], axis=1)[:, 0]
    p_e = jax.nn.softmax(e_in, axis=-1)
    top_p, top_i = lax.top_k(p_e, TOP_K)
    gate = p_g * top_p / jnp.sum(top_p, axis=-1, keepdims=True)
    e_idx = grp * EXPERTS_PER_GROUP + top_i
    y = routed_experts(t, e_idx, gate.astype(t.dtype), w_gate_up, w_down)
    return y.reshape(B, S, D)


def setup_inputs(seed: int = 0) -> dict:
    key = jax.random.key(seed)
    ks = iter(jax.random.split(key, 40))
    D = D_MODEL

    def dense(shape, fan_in):
        return jax.random.normal(next(ks), shape, jnp.float32) * fan_in ** -0.5

    def gain(shape):
        return 1.0 + 0.1 * jax.random.normal(next(ks), shape, jnp.float32)

    def small(shape):
        return 0.01 * jax.random.normal(next(ks), shape, jnp.float32)

    def moe_params(prefix):
        return {
            prefix + "norm_ffn": gain((D,)),
            prefix + "router_group": dense((D, N_GROUPS), D),
            prefix + "router_group_bias": small((N_GROUPS,)),
            prefix + "router_expert": dense((D, N_EXPERTS), D),
            prefix + "router_expert_bias": small((N_EXPERTS,)),
            prefix + "w_gate_up": dense((N_EXPERTS, D, 2 * D_EXPERT), D),
            prefix + "w_down": dense((N_EXPERTS, D_EXPERT, D), D_EXPERT),
        }

    x = jax.random.normal(next(ks), (BATCH, SEQ, D), jnp.float32)
    offsets = jax.random.randint(next(ks), (BATCH, 1), 0, 1024, dtype=jnp.int32)
    positions = offsets + jnp.arange(SEQ, dtype=jnp.int32)[None, :]
    inp = {"x": x, "positions": positions}
    inp["l0_norm_mix"] = gain((D,))
    inp["l0_mla_w_in"] = dense((D, MLA_Q_LORA + MLA_KV_LORA + MLA_ROPE), D)
    inp["l0_mla_q_lat_norm"] = gain((MLA_Q_LORA,))
    inp["l0_mla_kv_lat_norm"] = gain((MLA_KV_LORA,))
    inp["l0_mla_w_uq"] = dense((MLA_Q_LORA, MLA_HEADS * (MLA_NOPE + MLA_ROPE)), MLA_Q_LORA)
    inp["l0_mla_w_ukv"] = dense((MLA_KV_LORA, MLA_HEADS * (MLA_NOPE + MLA_V)), MLA_KV_LORA)
    inp["l0_mla_qk_gain"] = gain((2, MLA_NOPE + MLA_ROPE))
    inp["l0_mla_w_o"] = dense((MLA_HEADS * MLA_V, D), MLA_HEADS * MLA_V)
    inp.update(moe_params("l0_"))
    HD = FOX_HEADS * FOX_HEAD_DIM
    inp["l1_norm_mix"] = gain((D,))
    inp["l1_fox_w_in"] = dense((D, 3 * HD + FOX_HEADS + HD), D)
    inp["l1_fox_forget_bias"] = jax.random.uniform(next(ks), (FOX_HEADS,), jnp.float32, 1.0, 5.0)
    inp["l1_fox_qk_gain"] = gain((2, FOX_HEAD_DIM))
    inp["l1_fox_w_o"] = dense((HD, D), HD)
    inp.update(moe_params("l1_"))
    return inp


def reference(x, positions,
              l0_norm_mix, l0_mla_w_in, l0_mla_q_lat_norm, l0_mla_kv_lat_norm, l0_mla_w_uq,
              l0_mla_w_ukv, l0_mla_qk_gain, l0_mla_w_o,
              l0_norm_ffn, l0_router_group, l0_router_group_bias, l0_router_expert,
              l0_router_expert_bias, l0_w_gate_up, l0_w_down,
              l1_norm_mix, l1_fox_w_in, l1_fox_forget_bias, l1_fox_qk_gain, l1_fox_w_o,
              l1_norm_ffn, l1_router_group, l1_router_group_bias, l1_router_expert,
              l1_router_expert_bias, l1_w_gate_up, l1_w_down):
    mixers = [
        functools.partial(mla_mixer, positions=positions, w_in=l0_mla_w_in,
                          q_lat_norm=l0_mla_q_lat_norm, kv_lat_norm=l0_mla_kv_lat_norm,
                          w_uq=l0_mla_w_uq, w_ukv=l0_mla_w_ukv, qk_gain=l0_mla_qk_gain,
                          w_o=l0_mla_w_o),
        functools.partial(fox_mixer, w_in=l1_fox_w_in, forget_bias=l1_fox_forget_bias,
                          qk_gain=l1_fox_qk_gain, w_o=l1_fox_w_o),
    ]
    layers = [
        (l0_norm_mix, l0_norm_ffn, (l0_router_group, l0_router_group_bias, l0_router_expert,
                                     l0_router_expert_bias, l0_w_gate_up, l0_w_down)),
        (l1_norm_mix, l1_norm_ffn, (l1_router_group, l1_router_group_bias, l1_router_expert,
                                     l1_router_expert_bias, l1_w_gate_up, l1_w_down)),
    ]
    h = x
    for i in range(DEPTH):
        norm_mix, norm_ffn, moe_w = layers[i]
        h = h + mixers[i % N_MIXERS](rms_norm(h, norm_mix))
        h = h + hier_moe(rms_norm(h, norm_ffn), *moe_w)
    return h
```

```python
import functools
import math

import jax
import jax.numpy as jnp
from jax import lax
from jax.experimental import pallas as pl
from jax.experimental.pallas import tpu as pltpu

F32 = jnp.float32
BF16 = jnp.bfloat16
I32 = jnp.int32

EPS = 1e-6
ROPE_THETA = 10000.0
LANES = 128
HEAD_PAD = 256
N_GROUPS = 8
EXPERTS_PER_GROUP = 8
N_EXPERTS = N_GROUPS * EXPERTS_PER_GROUP
MOE_ROWS = 256
NEG_BIG = -1e30
VMEM_LIMIT = 52 * 1024 * 1024


def _params(sem, vmem=VMEM_LIMIT):
    return pltpu.CompilerParams(dimension_semantics=sem, vmem_limit_bytes=vmem)


def _rms(x, gain):
    ms = jnp.mean(x * x, axis=-1, keepdims=True)
    return x * lax.rsqrt(ms + EPS) * gain


def _lane_iota(shape):
    return lax.broadcasted_iota(I32, shape, len(shape) - 1)


def _rope_slab(x, cos, sin_signed):
    lane = _lane_iota((1, LANES))
    low = (lane % 64) < 32
    swapped = jnp.where(low, pltpu.roll(x, 96, 1), pltpu.roll(x, 32, 1))
    return x * cos + swapped * sin_signed


def _split3(x):
    a = x.astype(BF16)
    r = x - a.astype(F32)
    b = r.astype(BF16)
    c = (r - b.astype(F32)).astype(BF16)
    return a, b, c


def _mla_in_kernel(h_ref, pos_ref, gmix_ref, win_ref, gq_ref, gkv_ref, gkr_ref, freq_ref,
                   cq_ref, ckv_ref, kr_ref, cos_ref, sin_ref, *, q_lora, kv_lora):
    xn = _rms(h_ref[...], gmix_ref[...]).astype(BF16)
    z = jnp.dot(xn, win_ref[...], preferred_element_type=F32)
    cq_ref[...] = _rms(z[:, :q_lora], gq_ref[...]).astype(BF16)
    ckv_ref[...] = _rms(z[:, q_lora:q_lora + kv_lora], gkv_ref[...]).astype(BF16)
    kr = _rms(z[:, q_lora + kv_lora:], gkr_ref[...])
    slab = jnp.concatenate([kr, kr], axis=-1)
    ang = pos_ref[...].astype(F32) * freq_ref[...]
    lane = _lane_iota((1, LANES))
    cos = jnp.cos(ang)
    sin_signed = jnp.where((lane % 64) < 32, -jnp.sin(ang), jnp.sin(ang))
    cos_ref[...] = cos
    sin_ref[...] = sin_signed
    kr_ref[...] = _rope_slab(slab, cos, sin_signed).astype(BF16)


def _mla_q_kernel(cq_ref, w_ref, gn_ref, gr_ref, cos_ref, sin_ref, q_ref, *, heads, scale):
    q = jnp.dot(cq_ref[...], w_ref[...], preferred_element_type=F32)
    lane = _lane_iota((1, LANES))
    low = lane < 64
    cos = cos_ref[...]
    sin_signed = sin_ref[...]
    for h in range(heads):
        n = _rms(q[:, h * LANES:(h + 1) * LANES], gn_ref[...]) * scale
        q_ref[:, h * HEAD_PAD:h * HEAD_PAD + LANES] = n.astype(BF16)
    base = heads * LANES
    for p in range(heads // 2):
        slab = q[:, base + p * LANES:base + (p + 1) * LANES]
        sq = slab * slab
        ss0 = jnp.sum(jnp.where(low, sq, 0.0), axis=-1, keepdims=True)
        ss1 = jnp.sum(jnp.where(low, 0.0, sq), axis=-1, keepdims=True)
        inv = jnp.where(low, lax.rsqrt(ss0 / 64.0 + EPS), lax.rsqrt(ss1 / 64.0 + EPS))
        r = _rope_slab(slab * inv * gr_ref[...], cos, sin_signed) * scale
        zero = jnp.zeros_like(r)
        q_ref[:, (2 * p) * HEAD_PAD + LANES:(2 * p + 1) * HEAD_PAD] = jnp.where(low, r, zero).astype(BF16)
        q_ref[:, (2 * p + 1) * HEAD_PAD + LANES:(2 * p + 2) * HEAD_PAD] = jnp.where(low, zero, r).astype(BF16)


def _mla_kv_kernel(ckv_ref, w_ref, gn_ref, kr_ref, k_ref, v_ref, *, heads):
    kv = jnp.dot(ckv_ref[...], w_ref[...], preferred_element_type=F32)
    lane = _lane_iota((1, LANES))
    low = lane < 64
    kr = kr_ref[...]
    zero = jnp.zeros_like(kr)
    kr_even = jnp.where(low, kr, zero)
    kr_odd = jnp.where(low, zero, kr)
    for h in range(heads):
        n = _rms(kv[:, h * LANES:(h + 1) * LANES], gn_ref[...])
        k_ref[:, h * HEAD_PAD:h * HEAD_PAD + LANES] = n.astype(BF16)
        k_ref[:, h * HEAD_PAD + LANES:(h + 1) * HEAD_PAD] = kr_even if h % 2 == 0 else kr_odd
    v_ref[...] = kv[:, heads * LANES:].astype(BF16)


def _fox_decay_kernel(xn_ref, w_ref, b_ref, c_ref, carry_ref, *, ts):
    @pl.when(pl.program_id(1) == 0)
    def _():
        carry_ref[...] = jnp.zeros_like(carry_ref)

    f = jnp.dot(xn_ref[...], w_ref[...], preferred_element_type=F32) + b_ref[...]
    log_f = jnp.minimum(f, 0.0) - jnp.log1p(jnp.exp(-jnp.abs(f)))
    row = lax.broadcasted_iota(I32, (ts, ts), 0)
    col = lax.broadcasted_iota(I32, (ts, ts), 1)
    tri = jnp.where(col <= row, 1.0, 0.0).astype(BF16)
    a, b, c = _split3(log_f)
    parts = jnp.dot(tri, jnp.concatenate([a, b, c], axis=-1), preferred_element_type=F32)
    local = parts[:, :LANES] + parts[:, LANES:2 * LANES] + parts[:, 2 * LANES:]
    out = carry_ref[...] + local
    c_ref[...] = out
    carry_ref[...] = out[ts - 1:ts, :]


def _fox_proj_kernel(xn_ref, w_ref, g_ref, c_ref, o_ref, *, heads, mode, scale):
    z = jnp.dot(xn_ref[...], w_ref[...], preferred_element_type=F32)
    if mode == "v":
        o_ref[...] = z.astype(BF16)
        return
    if mode == "gate":
        o_ref[...] = z
        return
    lane = _lane_iota((1, LANES))
    cdec = c_ref[...]
    for h in range(heads):
        n = _rms(z[:, h * LANES:(h + 1) * LANES], g_ref[...]) * scale
        o_ref[:, h * HEAD_PAD:h * HEAD_PAD + LANES] = n.astype(BF16)
        ch = cdec[:, h:h + 1]
        if mode == "k":
            ch = -ch
        a, b, c = (t.astype(F32) for t in _split3(ch))
        if mode == "q":
            aug = jnp.where(lane == 0, a, jnp.where(lane == 1, b, jnp.where(lane == 2, c,
                  jnp.where(lane < 6, 1.0, 0.0))))
        else:
            aug = jnp.where(lane < 3, 1.0, jnp.where(lane == 3, a, jnp.where(lane == 4, b,
                  jnp.where(lane == 5, c, 0.0))))
        o_ref[:, h * HEAD_PAD + LANES:(h + 1) * HEAD_PAD] = aug.astype(BF16)


def _attn_kernel(q_ref, k_ref, v_ref, o_ref, m_ref, l_ref, acc_ref, *, tq):
    qi = pl.program_id(2)
    q = q_ref[...]
    m_ref[...] = jnp.full_like(m_ref, NEG_BIG)
    l_ref[...] = jnp.zeros_like(l_ref)
    acc_ref[...] = jnp.zeros_like(acc_ref)

    def tile(j, masked):
        start = pl.multiple_of(j * tq, tq)
        k = k_ref[pl.ds(start, tq), :]
        v = v_ref[pl.ds(start, tq), :]
        s = lax.dot_general(q, k, (((1,), (1,)), ((), ())), preferred_element_type=F32)
        if masked:
            row = lax.broadcasted_iota(I32, (tq, tq), 0)
            col = lax.broadcasted_iota(I32, (tq, tq), 1)
            s = jnp.where(row >= col, s, NEG_BIG)
        m_prev = m_ref[...]
        m_new = jnp.maximum(m_prev, jnp.max(s, axis=-1, keepdims=True))
        p = jnp.exp(s - m_new)
        alpha = jnp.exp(m_prev - m_new)
        l_ref[...] = alpha * l_ref[...] + jnp.sum(p, axis=-1, keepdims=True)
        acc_ref[...] = alpha * acc_ref[...] + jnp.dot(p.astype(BF16), v, preferred_element_type=F32)
        m_ref[...] = m_new

    def body(j, carry):
        tile(j, False)
        return carry

    lax.fori_loop(0, qi, body, 0)
    tile(qi, True)
    o_ref[...] = (acc_ref[...] / l_ref[...]).astype(BF16)


def _attention(q, k, v, *, batch, seq, heads, tq):
    nq = seq // tq
    kernel = functools.partial(_attn_kernel, tq=tq)
    return pl.pallas_call(
        kernel,
        grid=(batch, heads, nq),
        in_specs=[
            pl.BlockSpec((tq, HEAD_PAD), lambda b, h, i: (b * nq + i, h)),
            pl.BlockSpec((seq, HEAD_PAD), lambda b, h, i: (b, h)),
            pl.BlockSpec((seq, LANES), lambda b, h, i: (b, h)),
        ],
        out_specs=pl.BlockSpec((tq, LANES), lambda b, h, i: (b * nq + i, h)),
        out_shape=jax.ShapeDtypeStruct((batch * seq, heads * LANES), BF16),
        scratch_shapes=[pltpu.VMEM((tq, 1), F32), pltpu.VMEM((tq, 1), F32), pltpu.VMEM((tq, LANES), F32)],
        compiler_params=_params(("parallel", "parallel", "arbitrary")),
        name="causal_attention",
    )(q, k, v)


def _oproj_router_kernel(*refs, tm, gated):
    if gated:
        (o_ref, gate_ref, h_ref, wo_ref, gffn_ref, wr_ref, br_ref,
         h1_ref, xn_ref, route_ref, cnt_ref, carry_ref) = refs
    else:
        (o_ref, h_ref, wo_ref, gffn_ref, wr_ref, br_ref,
         h1_ref, xn_ref, route_ref, cnt_ref, carry_ref) = refs

    @pl.when(pl.program_id(0) == 0)
    def _():
        carry_ref[...] = jnp.zeros_like(carry_ref)

    o = o_ref[...]
    if gated:
        o = (o.astype(F32) * jax.nn.sigmoid(gate_ref[...])).astype(BF16)
    h1 = h_ref[...] + jnp.dot(o, wo_ref[...], preferred_element_type=F32)
    h1_ref[...] = h1
    xn = _rms(h1, gffn_ref[...])
    xn_ref[...] = xn

    x_hi = xn.astype(BF16)
    x_lo = (xn - x_hi.astype(F32)).astype(BF16)
    w = wr_ref[...]
    w_hi = w.astype(BF16)
    w_lo = (w - w_hi.astype(F32)).astype(BF16)
    logits = (jnp.dot(x_hi, w_hi, preferred_element_type=F32)
              + jnp.dot(x_hi, w_lo, preferred_element_type=F32)
              + jnp.dot(x_lo, w_hi, preferred_element_type=F32)) + br_ref[...]

    lane_i = _lane_iota((tm, LANES))
    lane = lane_i.astype(F32)
    big = 1e6
    is_g = (lane_i >= N_EXPERTS) & (lane_i < N_EXPERTS + N_GROUPS)
    gl = jnp.where(is_g, logits, NEG_BIG)
    gmax = jnp.max(gl, axis=-1, keepdims=True)
    grp = jnp.min(jnp.where(gl == gmax, lane, big), axis=-1, keepdims=True) - N_EXPERTS
    p_g = 1.0 / jnp.sum(jnp.where(is_g, jnp.exp(gl - gmax), 0.0), axis=-1, keepdims=True)

    in_grp = (lane_i // EXPERTS_PER_GROUP).astype(F32) == grp
    el = jnp.where(in_grp, logits, NEG_BIG)
    emax = jnp.max(el, axis=-1, keepdims=True)
    pe = jnp.where(in_grp, jnp.exp(el - emax), 0.0)
    pe = pe / jnp.sum(pe, axis=-1, keepdims=True)
    pm = jnp.where(in_grp, pe, -1.0)
    p1 = jnp.max(pm, axis=-1, keepdims=True)
    i1 = jnp.min(jnp.where(pm == p1, lane, big), axis=-1, keepdims=True)
    pm2 = jnp.where(lane == i1, -1.0, pm)
    p2 = jnp.max(pm2, axis=-1, keepdims=True)
    i2 = jnp.min(jnp.where(pm2 == p2, lane, big), axis=-1, keepdims=True)
    den = p1 + p2
    g1 = p_g * p1 / den
    g2 = p_g * p2 / den

    hit1 = lane == i1
    hit2 = lane == i2
    oh1 = jnp.where(hit1, 1.0, 0.0)
    oh2 = jnp.where(hit2, 1.0, 0.0)
    row = lax.broadcasted_iota(I32, (tm, tm), 0)
    col = lax.broadcasted_iota(I32, (tm, tm), 1)
    tri = jnp.where(col < row, 1.0, 0.0).astype(BF16)
    cum1 = jnp.dot(tri, oh1.astype(BF16), preferred_element_type=F32)
    cum2 = jnp.dot(tri, oh2.astype(BF16), preferred_element_type=F32)
    tot1 = jnp.sum(oh1, axis=0, keepdims=True)
    tot2 = jnp.sum(oh2, axis=0, keepdims=True)
    carry = carry_ref[...]
    r1 = jnp.sum(jnp.where(hit1, carry + cum1, 0.0), axis=-1, keepdims=True)
    r2 = jnp.sum(jnp.where(hit2, carry + tot1 + cum2, 0.0), axis=-1, keepdims=True)
    new_carry = carry + tot1 + tot2
    carry_ref[...] = new_carry
    cnt_ref[...] = new_carry

    zero = jnp.zeros((tm, LANES), F32)
    route = jnp.where(lane_i == 0, i1, jnp.where(lane_i == 1, i2,
            jnp.where(lane_i == 2, r1, jnp.where(lane_i == 3, r2,
            jnp.where(lane_i == 4, g1, jnp.where(lane_i == 5, g2, zero))))))
    route_ref[...] = route


def _oproj_router(o, gate, h, wo, gffn, wr, br, *, tm):
    tokens, d = h.shape
    gated = gate is not None
    row_spec = pl.BlockSpec((tm, d), lambda i: (i, 0))
    const = lambda shape: pl.BlockSpec(shape, lambda i: (0, 0))
    in_specs = [row_spec] + ([row_spec] if gated else []) + [
        row_spec, const(wo.shape), const((1, d)), const(wr.shape), const((1, LANES))]
    args = [o] + ([gate] if gated else []) + [h, wo, gffn, wr, br]
    return pl.pallas_call(
        functools.partial(_oproj_router_kernel, tm=tm, gated=gated),
        grid=(tokens // tm,),
        in_specs=in_specs,
        out_specs=[row_spec, row_spec, pl.BlockSpec((tm, LANES), lambda i: (i, 0)), const((1, LANES))],
        out_shape=[jax.ShapeDtypeStruct((tokens, d), F32), jax.ShapeDtypeStruct((tokens, d), F32),
                   jax.ShapeDtypeStruct((tokens, LANES), F32), jax.ShapeDtypeStruct((1, LANES), F32)],
        scratch_shapes=[pltpu.VMEM((1, LANES), F32)],
        compiler_params=_params(("arbitrary",)),
        name="oproj_router",
    )(*args)


def _plan_kernel(cnt_ref, plan_ref, blk_ref, *, n_blocks):
    cnt = cnt_ref[...]
    nblk = jnp.floor((cnt + (MOE_ROWS - 1)) * (1.0 / MOE_ROWS))
    row = lax.broadcasted_iota(I32, (LANES, LANES), 0)
    col = lax.broadcasted_iota(I32, (LANES, LANES), 1)
    upper = jnp.where(row < col, 1.0, 0.0).astype(BF16)
    nb8 = jnp.broadcast_to(nblk, (8, LANES)).astype(BF16)
    bstart = jnp.dot(nb8, upper, preferred_element_type=F32)[0:1, :]
    bend = bstart + nblk
    lane = _lane_iota((1, LANES))
    total = jnp.sum(nblk, axis=-1, keepdims=True)
    sub = lax.broadcasted_iota(I32, (8, LANES), 0)
    plan = jnp.where(sub == 0, jnp.broadcast_to(bstart, (8, LANES)),
           jnp.where(sub == 1, jnp.broadcast_to(nblk, (8, LANES)),
           jnp.where(sub == 2, jnp.broadcast_to(total, (8, LANES)), 0.0)))
    plan_ref[...] = plan
    bidx = lax.broadcasted_iota(I32, (n_blocks, LANES), 0).astype(F32)
    done = jnp.where((_lane_iota((n_blocks, LANES)) < N_EXPERTS) & (bend <= bidx), 1.0, 0.0)
    last_used = jnp.max(jnp.where(nblk > 0.0, lane.astype(F32), 0.0), axis=-1, keepdims=True)
    e_of_b = jnp.minimum(jnp.sum(done, axis=-1, keepdims=True), last_used)
    blk_ref[...] = jnp.broadcast_to(e_of_b, (n_blocks, LANES))


def _dest_kernel(route_ref, plan_ref, dest_ref, *, tm):
    route = route_ref[...]
    bstart = plan_ref[0:1, :] * float(MOE_ROWS)
    lane = _lane_iota((tm, LANES)).astype(F32)
    e1 = route[:, 0:1]
    e2 = route[:, 1:2]
    d1 =jnp.sum(jnp.where(lane == e1, bstart, 0.0), axis=-1, keepdims=True) + route[:, 2:3]
    d2 = jnp.sum(jnp.where(lane == e2, bstart, 0.0), axis=-1, keepdims=True) + route[:, 3:4]
    dest_ref[...] = jnp.where(lane == 0, d1, jnp.where(lane == 1, d2, 0.0))


def _dispatch_kernel(dest_ref, bstart_ref, nblk_ref, nused_ref, xn_hbm, buf_hbm, zeros_ref, sem_zero, sem_rows,
                     *, td, n_blocks):
    step = pl.program_id(0)

    @pl.when(step == 0)
    def _():
        zeros_ref[...] = jnp.zeros_like(zeros_ref)

        def block_copy(blk):
            return pltpu.make_async_copy(
                zeros_ref, buf_hbm.at[pl.ds(pl.multiple_of(blk * MOE_ROWS, MOE_ROWS), MOE_ROWS), :], sem_zero)

        def zero_copy(e):
            return block_copy(bstart_ref[e] + nblk_ref[e] - 1)

        def start_tail(blk, c):
            block_copy(blk).start()
            return c

        def wait_tail(blk, c):
            block_copy(blk).wait()
            return c

        lax.fori_loop(nused_ref[0], n_blocks, start_tail, 0)
        lax.fori_loop(nused_ref[0], n_blocks, wait_tail, 0)

        def start(e, c):
            @pl.when(nblk_ref[e] > 0)
            def _():
                zero_copy(e).start()
            return c

        def wait(e, c):
            @pl.when(nblk_ref[e] > 0)
            def _():
                zero_copy(e).wait()
            return c

        lax.fori_loop(0, N_EXPERTS, start, 0)
        lax.fori_loop(0, N_EXPERTS, wait, 0)

    def row_copy(a):
        tok = step * td + a // 2
        return pltpu.make_async_copy(xn_hbm.at[pl.ds(tok, 1), :], buf_hbm.at[pl.ds(dest_ref[a], 1), :], sem_rows)

    def start_row(a, c):
        row_copy(a).start()
        return c

    def wait_row(a, c):
        row_copy(a).wait()
        return c

    lax.fori_loop(0, 2 * td, start_row, 0)
    lax.fori_loop(0, 2 * td, wait_row, 0)


def _expert_kernel(blk_ref, nused_ref, x_ref, wgu_ref, wdn_ref, y_ref, wgu_bf, wdn_bf, *, d_expert):
    b = pl.program_id(0)

    @pl.when(b < nused_ref[0])
    def _():
        prev = blk_ref[jnp.maximum(b - 1, 0)]
        first = (b == 0) | (blk_ref[b] != prev)

        @pl.when(first)
        def _():
            wgu_bf[...] = wgu_ref[...].astype(BF16)
            wdn_bf[...] = wdn_ref[...].astype(BF16)

        x = x_ref[...].astype(BF16)
        gu = jnp.dot(x, wgu_bf[...], preferred_element_type=F32)
        g = gu[:, :d_expert]
        act = (g * jax.nn.sigmoid(g)) * gu[:, d_expert:]
        y_ref[...] = jnp.dot(act.astype(BF16), wdn_bf[...], preferred_element_type=F32)

    @pl.when(b >= nused_ref[0])
    def _():
        y_ref[...] = jnp.zeros_like(y_ref)


def _combine_kernel(dest_ref, y_hbm, route_ref, h_ref, g_ref, out_ref, xn_ref, rows_ref, sem, *, tc, with_norm):
    def row_copy(a):
        return pltpu.make_async_copy(y_hbm.at[pl.ds(dest_ref[a], 1), :], rows_ref.at[a % 2, pl.ds(a // 2, 1), :], sem)

    def start_row(a, c):
        row_copy(a).start()
        return c

    def wait_row(a, c):
        row_copy(a).wait()
        return c

    lax.fori_loop(0, 2 * tc, start_row, 0)
    lax.fori_loop(0, 2 * tc, wait_row, 0)
    route = route_ref[...]
    out = h_ref[...] + (rows_ref[0] * route[:, 4:5] + rows_ref[1] * route[:, 5:6])
    out_ref[...] = out
    if with_norm:
        xn_ref[...] = _rms(out, g_ref[...]).astype(BF16)
    else:
        xn_ref[...] = jnp.zeros_like(xn_ref)


def _moe(xn, route, cnt, h1, w_gate_up, w_down, next_gain):
    tokens, d = xn.shape
    d_expert = w_down.shape[1]
    n_blocks = (2 * tokens + N_EXPERTS * (MOE_ROWS - 1) + MOE_ROWS - 1) // MOE_ROWS
    assert n_blocks <= 256, "block counts must stay exact in bf16"
    nb_pad = (n_blocks + 7) // 8 * 8

    plan, blk = pl.pallas_call(
        functools.partial(_plan_kernel, n_blocks=nb_pad),
        out_shape=[jax.ShapeDtypeStruct((8, LANES), F32), jax.ShapeDtypeStruct((nb_pad, LANES), F32)],
        name="moe_plan",
    )(cnt)

    tm = 512
    dest = pl.pallas_call(
        functools.partial(_dest_kernel, tm=tm),
        grid=(tokens // tm,),
        in_specs=[pl.BlockSpec((tm, LANES), lambda i: (i, 0)), pl.BlockSpec((8, LANES), lambda i: (0, 0))],
        out_specs=pl.BlockSpec((tm, LANES), lambda i: (i, 0)),
        out_shape=jax.ShapeDtypeStruct((tokens, LANES), F32),
        compiler_params=_params(("parallel",)),
        name="moe_dest",
    )(route, plan)

    dest_flat = dest[:, :2].astype(I32).reshape(-1)
    bstart = plan[0, :N_EXPERTS].astype(I32)
    nblk = plan[1, :N_EXPERTS].astype(I32)
    n_used = plan[2, :1].astype(I32)
    blk_e = blk[:n_blocks, 0].astype(I32)

    td = 256
    smem = pltpu.SMEM
    buf = pl.pallas_call(
        functools.partial(_dispatch_kernel, td=td, n_blocks=n_blocks),
        grid=(tokens // td,),
        in_specs=[
            pl.BlockSpec((2 * td,), lambda i: (i,), memory_space=smem),
            pl.BlockSpec((N_EXPERTS,), lambda i: (0,), memory_space=smem),
            pl.BlockSpec((N_EXPERTS,), lambda i: (0,), memory_space=smem),
            pl.BlockSpec((1,), lambda i: (0,), memory_space=smem),
            pl.BlockSpec(memory_space=pl.ANY),
        ],
        out_specs=pl.BlockSpec(memory_space=pl.ANY),
        out_shape=jax.ShapeDtypeStruct((n_blocks * MOE_ROWS, d), F32),
        scratch_shapes=[pltpu.VMEM((MOE_ROWS, d), F32), pltpu.SemaphoreType.DMA(()), pltpu.SemaphoreType.DMA(())],
        compiler_params=_params(("arbitrary",)),
        name="moe_dispatch",
    )(dest_flat, bstart, nblk, n_used, xn)

    y = pl.pallas_call(
        functools.partial(_expert_kernel, d_expert=d_expert),
        grid_spec=pltpu.PrefetchScalarGridSpec(
            num_scalar_prefetch=2,
            grid=(n_blocks,),
            in_specs=[
                pl.BlockSpec((MOE_ROWS, d), lambda b, be, nu: (b, 0)),
                pl.BlockSpec((None, d, 2 * d_expert), lambda b, be, nu: (be[b], 0, 0)),
                pl.BlockSpec((None, d_expert, d), lambda b, be, nu: (be[b], 0, 0)),
            ],
            out_specs=pl.BlockSpec((MOE_ROWS, d), lambda b, be, nu: (b, 0)),
            scratch_shapes=[pltpu.VMEM((d, 2 * d_expert), BF16), pltpu.VMEM((d_expert, d), BF16)],
        ),
        out_shape=jax.ShapeDtypeStruct((n_blocks * MOE_ROWS, d), F32),
        compiler_params=_params(("arbitrary",)),
        name="moe_experts",
    )(blk_e, n_used, buf, w_gate_up, w_down)

    tc = 256
    with_norm = next_gain is not None
    gain = next_gain if with_norm else jnp.ones((1, d), F32)
    row_spec = pl.BlockSpec((tc, d), lambda i: (i, 0))
    out, xn_next = pl.pallas_call(
        functools.partial(_combine_kernel, tc=tc, with_norm=with_norm),
        grid=(tokens // tc,),
        in_specs=[
            pl.BlockSpec((2 * tc,), lambda i: (i,), memory_space=smem),
            pl.BlockSpec(memory_space=pl.ANY),
            pl.BlockSpec((tc, LANES), lambda i: (i, 0)),
            row_spec,
            pl.BlockSpec((1, d), lambda i: (0, 0)),
        ],
        out_specs=[row_spec, row_spec if with_norm else pl.BlockSpec((8, LANES), lambda i: (0, 0))],
        out_shape=[jax.ShapeDtypeStruct((tokens, d), F32),
                   jax.ShapeDtypeStruct((tokens, d), BF16) if with_norm else jax.ShapeDtypeStruct((8, LANES), BF16)],
        scratch_shapes=[pltpu.VMEM((2, tc, d), F32), pltpu.SemaphoreType.DMA(())],
        compiler_params=_params(("arbitrary",)),
        name="moe_combine",
    )(dest_flat, y, route, h1, gain)
    return out, (xn_next if with_norm else None)


def _router_weights(w_group, b_group, w_expert, b_expert):
    d = w_group.shape[0]
    pad = LANES - N_EXPERTS - N_GROUPS
    w = jnp.concatenate([w_expert, w_group, jnp.zeros((d, pad), F32)], axis=1)
    b = jnp.concatenate([b_expert, b_group, jnp.zeros((pad,), F32)])[None, :]
    return w, b


def _row(v):
    return v.astype(F32)[None, :]


def kernel(x, positions, l0_norm_mix, l0_mla_w_in, l0_mla_q_lat_norm, l0_mla_kv_lat_norm, l0_mla_w_uq, l0_mla_w_ukv, l0_mla_qk_gain, l0_mla_w_o, l0_norm_ffn, l0_router_group, l0_router_group_bias, l0_router_expert, l0_router_expert_bias, l0_w_gate_up, l0_w_down, l1_norm_mix, l1_fox_w_in, l1_fox_forget_bias, l1_fox_qk_gain, l1_fox_w_o, l1_norm_ffn, l1_router_group, l1_router_group_bias, l1_router_expert, l1_router_expert_bias, l1_w_gate_up, l1_w_down):
    batch, seq, d = x.shape
    tokens = batch * seq
    q_lora = l0_mla_q_lat_norm.shape[0]
    kv_lora = l0_mla_kv_lat_norm.shape[0]
    rope = l0_mla_w_in.shape[1] - q_lora - kv_lora
    nope = l0_mla_qk_gain.shape[1] - rope
    heads = l0_mla_w_uq.shape[1] // (nope + rope)
    assert rope == 64 and nope == LANES and heads % 2 == 0
    tq = min(512, seq)

    h0 = x.reshape(tokens, d)
    pos = positions.reshape(tokens, 1)

    half = rope // 2
    freqs = ROPE_THETA ** (-jnp.arange(half, dtype=F32) / half)
    freq_row = jnp.tile(freqs, LANES // half)[None, :]
    gkr = _row(l0_mla_qk_gain[1, nope:])
    tm = 512
    row = lambda w: pl.BlockSpec((tm, w), lambda i: (i, 0))
    const = lambda shape: pl.BlockSpec(shape, lambda i: (0, 0))
    win = l0_mla_w_in.astype(BF16)
    cq, ckv, kr2, cos_t, sin_t = pl.pallas_call(
        functools.partial(_mla_in_kernel, q_lora=q_lora, kv_lora=kv_lora),
        grid=(tokens // tm,),
        in_specs=[row(d), row(1), const((1, d)), const(win.shape), const((1, q_lora)), const((1, kv_lora)),
                  const((1, rope)), const((1, LANES))],
        out_specs=[row(q_lora), row(kv_lora), row(LANES), row(LANES), row(LANES)],
        out_shape=[jax.ShapeDtypeStruct((tokens, q_lora), BF16), jax.ShapeDtypeStruct((tokens, kv_lora), BF16),
                   jax.ShapeDtypeStruct((tokens, LANES), BF16), jax.ShapeDtypeStruct((tokens, LANES), F32),
                   jax.ShapeDtypeStruct((tokens, LANES), F32)],
        compiler_params=_params(("parallel",)),
        name="mla_in",
    )(h0, pos, _row(l0_norm_mix), win, _row(l0_mla_q_lat_norm), _row(l0_mla_kv_lat_norm), gkr, freq_row)

    wuq = l0_mla_w_uq.reshape(q_lora, heads, nope + rope)
    wuq = jnp.concatenate([wuq[:, :, :nope].reshape(q_lora, heads * nope),
                           wuq[:, :, nope:].reshape(q_lora, heads * rope)], axis=1).astype(BF16)
    scale = 1.0 / math.sqrt(nope + rope)
    q_aug = pl.pallas_call(
        functools.partial(_mla_q_kernel, heads=heads, scale=scale),
        grid=(tokens // tm,),
        in_specs=[row(q_lora), const(wuq.shape), const((1, nope)), const((1, LANES)), row(LANES), row(LANES)],
        out_specs=row(heads * HEAD_PAD),
        out_shape=jax.ShapeDtypeStruct((tokens, heads * HEAD_PAD), BF16),
        compiler_params=_params(("parallel",)),
        name="mla_q",
    )(cq, wuq, _row(l0_mla_qk_gain[0, :nope]), _row(jnp.tile(l0_mla_qk_gain[0, nope:], 2)), cos_t, sin_t)

    v_dim = l0_mla_w_ukv.shape[1] // heads - nope
    assert v_dim == LANES
    wukv = l0_mla_w_ukv.reshape(kv_lora, heads, nope + v_dim)
    wukv = jnp.concatenate([wukv[:, :, :nope].reshape(kv_lora, heads * nope),
                            wukv[:, :, nope:].reshape(kv_lora, heads * v_dim)], axis=1).astype(BF16)
    k_aug, v0 = pl.pallas_call(
        functools.partial(_mla_kv_kernel, heads=heads),
        grid=(tokens // tm,),
        in_specs=[row(kv_lora), const(wukv.shape), const((1, nope)), row(LANES)],
        out_specs=[row(heads * HEAD_PAD), row(heads * v_dim)],
        out_shape=[jax.ShapeDtypeStruct((tokens, heads * HEAD_PAD), BF16),
                   jax.ShapeDtypeStruct((tokens, heads * v_dim), BF16)],
        compiler_params=_params(("parallel",)),
        name="mla_kv",
    )(ckv, wukv, _row(l0_mla_qk_gain[1, :nope]), kr2)

    o0 = _attention(q_aug, k_aug, v0, batch=batch, seq=seq, heads=heads, tq=tq)

    wr0, br0 = _router_weights(l0_router_group, l0_router_group_bias, l0_router_expert, l0_router_expert_bias)
    h1, xn1, route0, cnt0 = _oproj_router(o0, None, h0, l0_mla_w_o.astype(BF16), _row(l0_norm_ffn), wr0, br0, tm=256)
    h2, xn2 = _moe(xn1, route0, cnt0, h1, l0_w_gate_up, l0_w_down, _row(l1_norm_mix))

    fheads = l1_fox_forget_bias.shape[0]
    hd = l1_fox_qk_gain.shape[1]
    assert hd == LANES
    hh = fheads * hd
    w1 = l1_fox_w_in
    wq1, wk1, wv1 = (w1[:, i * hh:(i + 1) * hh].astype(BF16) for i in range(3))
    wf1 = jnp.concatenate([w1[:, 3 * hh:3 * hh + fheads], jnp.zeros((d, LANES - fheads), F32)], axis=1).astype(BF16)
    wg1 = w1[:, 3 * hh + fheads:].astype(BF16)
    fbias = jnp.concatenate([l1_fox_forget_bias.astype(F32), jnp.zeros((LANES - fheads,), F32)])[None, :]

    ts = min(512, seq)
    ns = seq // ts
    cdec = pl.pallas_call(
        functools.partial(_fox_decay_kernel, ts=ts),
        grid=(batch, ns),
        in_specs=[pl.BlockSpec((ts, d), lambda b, i: (b * ns + i, 0)), pl.BlockSpec((d, LANES), lambda b, i: (0, 0)),
                  pl.BlockSpec((1, LANES), lambda b, i: (0, 0))],
        out_specs=pl.BlockSpec((ts, LANES), lambda b, i: (b * ns + i, 0)),
        out_shape=jax.ShapeDtypeStruct((tokens, LANES), F32),
        scratch_shapes=[pltpu.VMEM((1, LANES), F32)],
        compiler_params=_params(("arbitrary", "arbitrary")),
        name="fox_decay",
    )(xn2, wf1, fbias)

    def fox_proj(w, gain, mode, out_width, out_dtype):
        return pl.pallas_call(
            functools.partial(_fox_proj_kernel, heads=fheads, mode=mode,
                              scale=(1.0 / math.sqrt(hd)) if mode == "q" else 1.0),
            grid=(tokens // tm,),
            in_specs=[row(d), const(w.shape), const((1, hd)), row(LANES)],
            out_specs=row(out_width),
            out_shape=jax.ShapeDtypeStruct((tokens, out_width), out_dtype),
            compiler_params=_params(("parallel",)),
            name="fox_proj_" + mode,
        )(xn2, w, gain, cdec)

    q1 = fox_proj(wq1, _row(l1_fox_qk_gain[0]), "q", fheads * HEAD_PAD, BF16)
    k1 = fox_proj(wk1, _row(l1_fox_qk_gain[1]), "k", fheads * HEAD_PAD, BF16)
    v1 = fox_proj(wv1, _row(l1_fox_qk_gain[0]), "v", hh, BF16)
    gate1 = fox_proj(wg1, _row(l1_fox_qk_gain[0]), "gate", hh, F32)

    o1 = _attention(q1, k1, v1, batch=batch, seq=seq, heads=fheads, tq=tq)

    wr1, br1 = _router_weights(l1_router_group, l1_router_group_bias, l1_router_expert, l1_router_expert_bias)
    h3, xn3, route1, cnt1 = _oproj_router(o1, gate1, h2, l1_fox_w_o.astype(BF16), _row(l1_norm_ffn), wr1, br1, tm=256)
    h4, _ = _moe(xn3, route1, cnt1, h3, l1_w_gate_up, l1_w_down, None)
    return h4.reshape(batch, seq, d)
```

```python
import functools
import math

import jax
import jax.numpy as jnp
from jax import lax
from jax.experimental import pallas as pl
from jax.experimental.pallas import tpu as pltpu

F32 = jnp.float32
BF16 = jnp.bfloat16
I32 = jnp.int32

EPS = 1e-6
ROPE_THETA = 10000.0
LANES = 128
HEAD_PAD = 256
N_GROUPS = 8
EXPERTS_PER_GROUP = 8
N_EXPERTS = N_GROUPS * EXPERTS_PER_GROUP
MOE_ROWS = 256
NEG_BIG = -1e30
LOG2E = math.log2(math.e)
VMEM_LIMIT = 52 * 1024 * 1024


def _params(sem, vmem=VMEM_LIMIT):
    return pltpu.CompilerParams(dimension_semantics=sem, vmem_limit_bytes=vmem)


def _rms(x, gain):
    ms = jnp.mean(x * x, axis=-1, keepdims=True)
    return x * lax.rsqrt(ms + EPS) * gain


def _lane_iota(shape):
    return lax.broadcasted_iota(I32, shape, len(shape) - 1)


def _rope_slab(x, cos, sin_signed):
    lane = _lane_iota((1, LANES))
    low = (lane % 64) < 32
    swapped = jnp.where(low, pltpu.roll(x, 96, 1), pltpu.roll(x, 32, 1))
    return x * cos + swapped * sin_signed


def _split3(x):
    a = x.astype(BF16)
    r = x - a.astype(F32)
    b = r.astype(BF16)
    c = (r - b.astype(F32)).astype(BF16)
    return a, b, c


def _mla_in_kernel(h_ref, pos_ref, gmix_ref, win_ref, gq_ref, gkv_ref, gkr_ref, freq_ref,
                   cq_ref, ckv_ref, kr_ref, cos_ref, sin_ref, *, q_lora, kv_lora):
    xn = _rms(h_ref[...], gmix_ref[...]).astype(BF16)
    z = jnp.dot(xn, win_ref[...], preferred_element_type=F32)
    cq_ref[...] = _rms(z[:, :q_lora], gq_ref[...]).astype(BF16)
    ckv_ref[...] = _rms(z[:, q_lora:q_lora + kv_lora], gkv_ref[...]).astype(BF16)
    kr = _rms(z[:, q_lora + kv_lora:], gkr_ref[...])
    slab = jnp.concatenate([kr, kr], axis=-1)
    ang = pos_ref[...].astype(F32) * freq_ref[...]
    lane = _lane_iota((1, LANES))
    cos = jnp.cos(ang)
    sin_signed = jnp.where((lane % 64) < 32, -jnp.sin(ang), jnp.sin(ang))
    cos_ref[...] = cos
    sin_ref[...] = sin_signed
    kr_ref[...] = _rope_slab(slab, cos, sin_signed).astype(BF16)


def _mla_q_kernel(cq_ref, w_ref, gn_ref, gr_ref, cos_ref, sin_ref, q_ref, *, heads, scale):
    q = jnp.dot(cq_ref[...], w_ref[...], preferred_element_type=F32)
    lane = _lane_iota((1, LANES))
    low = lane < 64
    cos = cos_ref[...]
    sin_signed = sin_ref[...]
    for h in range(heads):
        n = _rms(q[:, h * LANES:(h + 1) * LANES], gn_ref[...]) * scale
        q_ref[:, h * HEAD_PAD:h * HEAD_PAD + LANES] = n.astype(BF16)
    base = heads * LANES
    for p in range(heads // 2):
        slab = q[:, base + p * LANES:base + (p + 1) * LANES]
        sq = slab * slab
        ss0 = jnp.sum(jnp.where(low, sq, 0.0), axis=-1, keepdims=True)
        ss1 = jnp.sum(jnp.where(low, 0.0, sq), axis=-1, keepdims=True)
        inv = jnp.where(low, lax.rsqrt(ss0 / 64.0 + EPS), lax.rsqrt(ss1 / 64.0 + EPS))
        r = _rope_slab(slab * inv * gr_ref[...], cos, sin_signed) * scale
        zero = jnp.zeros_like(r)
        q_ref[:, (2 * p) * HEAD_PAD + LANES:(2 * p + 1) * HEAD_PAD] = jnp.where(low, r, zero).astype(BF16)
        q_ref[:, (2 * p + 1) * HEAD_PAD + LANES:(2 * p + 2) * HEAD_PAD] = jnp.where(low, zero, r).astype(BF16)


def _mla_kv_kernel(ckv_ref, w_ref, gn_ref, kr_ref, k_ref, v_ref, *, heads):
    kv = jnp.dot(ckv_ref[...], w_ref[...], preferred_element_type=F32)
    lane = _lane_iota((1, LANES))
    low = lane < 64
    kr = kr_ref[...]
    zero = jnp.zeros_like(kr)
    kr_even = jnp.where(low, kr, zero)
    kr_odd = jnp.where(low, zero, kr)
    for h in range(heads):
        n = _rms(kv[:, h * LANES:(h + 1) * LANES], gn_ref[...])
        k_ref[:, h * HEAD_PAD:h * HEAD_PAD + LANES] = n.astype(BF16)
        k_ref[:, h * HEAD_PAD + LANES:(h + 1) * HEAD_PAD] = kr_even if h % 2 == 0 else kr_odd
    v_ref[...] = kv[:, heads * LANES:].astype(BF16)


def _fox_decay_kernel(xn_ref, w_ref, b_ref, c_ref, carry_ref, *, ts):
    @pl.when(pl.program_id(1) == 0)
    def _():
        carry_ref[...] = jnp.zeros_like(carry_ref)

    f = jnp.dot(xn_ref[...], w_ref[...], preferred_element_type=F32) + b_ref[...]
    log_f = jnp.minimum(f, 0.0) - jnp.log1p(jnp.exp(-jnp.abs(f)))
    row = lax.broadcasted_iota(I32, (ts, ts), 0)
    col = lax.broadcasted_iota(I32, (ts, ts), 1)
    tri = jnp.where(col <= row, 1.0, 0.0).astype(BF16)
    a, b, c = _split3(log_f)
    parts = jnp.dot(tri, jnp.concatenate([a, b, c], axis=-1), preferred_element_type=F32)
    local = parts[:, :LANES] + parts[:, LANES:2 * LANES] + parts[:, 2 * LANES:]
    out = carry_ref[...] + local
    c_ref[...] = out
    carry_ref[...] = out[ts - 1:ts, :]


def _fox_proj_kernel(xn_ref, w_ref, g_ref, c_ref, o_ref, *, heads, mode, scale):
    z = jnp.dot(xn_ref[...], w_ref[...], preferred_element_type=F32)
    if mode == "v":
        o_ref[...] = z.astype(BF16)
        return
    if mode == "gate":
        o_ref[...] = z
        return
    lane = _lane_iota((1, LANES))
    cdec = c_ref[...] * LOG2E
    for h in range(heads):
        n = _rms(z[:, h * LANES:(h + 1) * LANES], g_ref[...]) * scale
        o_ref[:, h * HEAD_PAD:h * HEAD_PAD + LANES] = n.astype(BF16)
        ch = cdec[:, h:h + 1]
        if mode == "k":
            ch = -ch
        a, b, c = (t.astype(F32) for t in _split3(ch))
        if mode == "q":
            aug = jnp.where(lane == 0, a, jnp.where(lane == 1, b, jnp.where(lane == 2, c,
                  jnp.where(lane < 6, 1.0, 0.0))))
        else:
            aug = jnp.where(lane < 3, 1.0, jnp.where(lane == 3, a, jnp.where(lane == 4, b,
                  jnp.where(lane == 5, c, 0.0))))
        o_ref[:, h * HEAD_PAD + LANES:(h + 1) * HEAD_PAD] = aug.astype(BF16)


def _attn_tile(q, k_ref, v_ref, j, m_ref, l_ref, acc_ref, *, tk, masked):
    rows = q.shape[0]
    start = pl.multiple_of(j * tk, tk)
    k = k_ref[pl.ds(start, tk), :]
    v = v_ref[pl.ds(start, tk), :]
    s = lax.dot_general(q, k, (((1,), (1,)), ((), ())), preferred_element_type=F32)
    if masked:
        row = lax.broadcasted_iota(I32, (rows, tk), 0)
        col = lax.broadcasted_iota(I32, (rows, tk), 1)
        s = jnp.where(row >= col, s, NEG_BIG)
    m_prev = m_ref[...]
    m_new = jnp.maximum(m_prev, jnp.max(s, axis=-1, keepdims=True))
    alpha = jnp.exp2(m_prev - m_new)
    p = jnp.exp2(s - jnp.concatenate([m_new] * (tk // LANES), axis=1))
    psum = p[:, :LANES]
    for c in range(1, tk // LANES):
        psum = psum + p[:, c * LANES:(c + 1) * LANES]
    l_ref[...] = alpha * l_ref[...] + psum
    acc_ref[...] = alpha * acc_ref[...] + jnp.dot(p.astype(BF16), v, preferred_element_type=F32)
    m_ref[...] = m_new


def _attn_kernel(q_ref, k_ref, v_ref, o_ref, m_a, l_a, acc_a, m_b, l_b, acc_b, *, tk):
    qi = pl.program_id(2)
    for m, l, acc in ((m_a, l_a, acc_a), (m_b, l_b, acc_b)):
        m[...] = jnp.full_like(m, NEG_BIG)
        l[...] = jnp.zeros_like(l)
        acc[...] = jnp.zeros_like(acc)
    q = q_ref[...]
    tile = functools.partial(_attn_tile, k_ref=k_ref, v_ref=v_ref, tk=tk)

    def body(jj, carry):
        tile(q, j=2 * jj, m_ref=m_a, l_ref=l_a, acc_ref=acc_a, masked=False)
        tile(q, j=2 * jj + 1, m_ref=m_b, l_ref=l_b, acc_ref=acc_b, masked=False)
        return carry

    lax.fori_loop(0, qi, body, 0)
    tile(q, j=2 * qi, m_ref=m_a, l_ref=l_a, acc_ref=acc_a, masked=True)
    low = pl.ds(tk, tk)
    tile(q_ref[low, :], j=2 * qi + 1, m_ref=m_b.at[low, :], l_ref=l_b.at[low, :], acc_ref=acc_b.at[low, :],
         masked=True)
    m = jnp.maximum(m_a[...], m_b[...])
    w_a = jnp.exp2(m_a[...] - m)
    w_b = jnp.exp2(m_b[...] - m)
    l = jnp.sum(w_a * l_a[...] + w_b * l_b[...], axis=-1, keepdims=True)
    o_ref[...] = ((w_a * acc_a[...] + w_b * acc_b[...]) / l).astype(BF16)


def _attention(q, k, v, *, batch, seq, heads, tk):
    tq = 2 * tk
    nq = seq // tq
    stat = pltpu.VMEM((tq, LANES), F32)
    return pl.pallas_call(
        functools.partial(_attn_kernel, tk=tk),
        grid=(batch, heads, nq),
        in_specs=[
            pl.BlockSpec((tq, HEAD_PAD), lambda b, h, i: (b * nq + i, h)),
            pl.BlockSpec((seq, HEAD_PAD), lambda b, h, i: (b, h)),
            pl.BlockSpec((seq, LANES), lambda b, h, i: (b, h)),
        ],
        out_specs=pl.BlockSpec((tq, LANES), lambda b, h, i: (b * nq + i, h)),
        out_shape=jax.ShapeDtypeStruct((batch * seq, heads * LANES), BF16),
        scratch_shapes=[stat] * 6,
        compiler_params=_params(("parallel", "parallel", "arbitrary")),
        name="causal_attention",
    )(q, k, v)


def _oproj_router_kernel(*refs, tm, gated):
    if gated:
        (o_ref, gate_ref, h_ref, wo_ref, gffn_ref, wr_ref, br_ref,
         h1_ref, xn_ref, route_ref, cnt_ref, carry_ref) = refs
    else:
        (o_ref, h_ref, wo_ref, gffn_ref, wr_ref, br_ref,
         h1_ref, xn_ref, route_ref, cnt_ref, carry_ref) = refs

    @pl.when(pl.program_id(0) == 0)
    def _():
        carry_ref[...] = jnp.zeros_like(carry_ref)

    o = o_ref[...]
    if gated:
        o = (o.astype(F32) * jax.nn.sigmoid(gate_ref[...])).astype(BF16)
    h1 = h_ref[...] + jnp.dot(o, wo_ref[...], preferred_element_type=F32)
    h1_ref[...] = h1
    xn = _rms(h1, gffn_ref[...])
    xn_ref[...] = xn

    x_hi = xn.astype(BF16)
    x_lo = (xn - x_hi.astype(F32)).astype(BF16)
    w = wr_ref[...]
    w_hi = w.astype(BF16)
    w_lo = (w - w_hi.astype(F32)).astype(BF16)
    logits = (jnp.dot(x_hi, w_hi, preferred_element_type=F32)
              + jnp.dot(x_hi, w_lo, preferred_element_type=F32)
              + jnp.dot(x_lo, w_hi, preferred_element_type=F32)) + br_ref[...]

    lane_i = _lane_iota((tm, LANES))
    lane = lane_i.astype(F32)
    big = 1e6
    is_g = (lane_i >= N_EXPERTS) & (lane_i < N_EXPERTS + N_GROUPS)
    gl = jnp.where(is_g, logits, NEG_BIG)
    gmax = jnp.max(gl, axis=-1, keepdims=True)
    grp = jnp.min(jnp.where(gl == gmax, lane, big), axis=-1, keepdims=True) - N_EXPERTS
    p_g = 1.0 / jnp.sum(jnp.where(is_g, jnp.exp(gl - gmax), 0.0), axis=-1, keepdims=True)

    in_grp = (lane_i // EXPERTS_PER_GROUP).astype(F32) == grp
    el = jnp.where(in_grp, logits, NEG_BIG)
    emax = jnp.max(el, axis=-1, keepdims=True)
    pe = jnp.where(in_grp, jnp.exp(el - emax), 0.0)
    pe = pe / jnp.sum(pe, axis=-1, keepdims=True)
    pm = jnp.where(in_grp, pe, -1.0)
    p1 = jnp.max(pm, axis=-1, keepdims=True)
    i1 = jnp.min(jnp.where(pm == p1, lane, big), axis=-1, keepdims=True)
    pm2 = jnp.where(lane == i1, -1.0, pm)
    p2 = jnp.max(pm2, axis=-1, keepdims=True)
    i2 = jnp.min(jnp.where(pm2 == p2, lane, big), axis=-1, keepdims=True)
    den = p1 + p2
    g1 = p_g * p1 / den
    g2 = p_g * p2 / den

    hit1 = lane == i1
    hit2 = lane == i2
    oh1 = jnp.where(hit1, 1.0, 0.0)
    oh2 = jnp.where(hit2, 1.0, 0.0)
    row = lax.broadcasted_iota(I32, (tm, tm), 0)
    col = lax.broadcasted_iota(I32, (tm, tm), 1)
    tri = jnp.where(col < row, 1.0, 0.0).astype(BF16)
    cum1 = jnp.dot(tri, oh1.astype(BF16), preferred_element_type=F32)
    cum2 = jnp.dot(tri, oh2.astype(BF16), preferred_element_type=F32)
    tot1 = jnp.sum(oh1, axis=0, keepdims=True)
    tot2 = jnp.sum(oh2, axis=0, keepdims=True)
    carry = carry_ref[...]
    r1 = jnp.sum(jnp.where(hit1, carry + cum1, 0.0), axis=-1, keepdims=True)
    r2 = jnp.sum(jnp.where(hit2, carry + tot1 + cum2, 0.0), axis=-1, keepdims=True)
    new_carry = carry + tot1 + tot2
    carry_ref[...] = new_carry
    cnt_ref[...] = new_carry

    zero = jnp.zeros((tm, LANES), F32)
    route = jnp.where(lane_i == 0, i1, jnp.where(lane_i == 1, i2,
            jnp.where(lane_i == 2, r1, jnp.where(lane_i == 3, r2,
            jnp.where(lane_i == 4, g1, jnp.where(lane_i == 5, g2, zero))))))
    route_ref[...] = route


def _oproj_router(o, gate, h, wo, gffn, wr, br, *, tm):
    tokens, d = h.shape
    gated = gate is not None
    row_spec = pl.BlockSpec((tm, d), lambda i: (i, 0))
    const = lambda shape: pl.BlockSpec(shape, lambda i: (0, 0))
    in_specs = [row_spec] + ([row_spec] if gated else []) + [
        row_spec, const(wo.shape), const((1, d)), const(wr.shape), const((1, LANES))]
    args = [o] + ([gate] if gated else []) + [h, wo, gffn, wr, br]
    return pl.pallas_call(
        functools.partial(_oproj_router_kernel, tm=tm, gated=gated),
        grid=(tokens // tm,),
        in_specs=in_specs,
        out_specs=[row_spec, row_spec, pl.BlockSpec((tm, LANES), lambda i: (i, 0)), const((1, LANES))],
        out_shape=[jax.ShapeDtypeStruct((tokens, d), F32), jax.ShapeDtypeStruct((tokens, d), F32),
                   jax.ShapeDtypeStruct((tokens, LANES), F32), jax.ShapeDtypeStruct((1, LANES), F32)],
        scratch_shapes=[pltpu.VMEM((1, LANES), F32)],
        compiler_params=_params(("arbitrary",)),
        name="oproj_router",
    )(*args)


def _plan_kernel(cnt_ref, plan_ref, blk_ref, *, n_blocks):
    cnt = cnt_ref[...]
    nblk = jnp.floor((cnt + (MOE_ROWS - 1)) * (1.0 / MOE_ROWS))
    row = lax.broadcasted_iota(I32, (LANES, LANES), 0)
    col = lax.broadcasted_iota(I32, (LANES, LANES), 1)
    upper = jnp.where(row < col, 1.0, 0.0).astype(BF16)
    nb8 = jnp.broadcast_to(nblk, (8, LANES)).astype(BF16)
    bstart = jnp.dot(nb8, upper, preferred_element_type=F32)[0:1, :]
    bend = bstart + nblk
    lane = _lane_iota((1, LANES))
    total = jnp.sum(nblk, axis=-1, keepdims=True)
    sub = lax.broadcasted_iota(I32, (8, LANES), 0)
    plan = jnp.where(sub == 0, jnp.broadcast_to(bstart, (8, LANES)),
           jnp.where(sub == 1, jnp.broadcast_to(nblk, (8, LANES)),
           jnp.where(sub == 2, jnp.broadcast_to(total, (8, LANES)), 0.0)))
    plan_ref[...] = plan
    bidx = lax.broadcasted_iota(I32, (n_blocks, LANES), 0).astype(F32)
    done = jnp.where((_lane_iota((n_blocks, LANES)) < N_EXPERTS) & (bend <= bidx), 1.0, 0.0)
    last_used = jnp.max(jnp.where(nblk > 0.0, lane.astype(F32), 0.0), axis=-1, keepdims=True)
    e_of_b = jnp.minimum(jnp.sum(done, axis=-1, keepdims=True), last_used)
    blk_ref[...] = jnp.broadcast_to(e_of_b, (n_blocks, LANES))


def _dest_kernel(route_ref, plan_ref, dest_ref, *, tm):
    route = route_ref[...]
    bstart = plan_ref[0:1, :] * float(MOE_ROWS)
    lane = _lane_iota((tm, LANES)).astype(F32)
    e1 = route[:, 0:1]
    e2 = route[:, 1:2]
    d1 =jnp.sum(jnp.where(lane == e1, bstart, 0.0), axis=-1, keepdims=True) + route[:, 2:3]
    d2 = jnp.sum(jnp.where(lane == e2, bstart, 0.0), axis=-1, keepdims=True) + route[:, 3:4]
    dest_ref[...] = jnp.where(lane == 0, d1, jnp.where(lane == 1, d2, 0.0))


def _dispatch_kernel(dest_ref, bstart_ref, nblk_ref, nused_ref, x_ref, buf_hbm, zeros_ref, sem_zero, sem_rows,
                     *, td, n_blocks):
    step = pl.program_id(0)

    @pl.when(step == 0)
    def _():
        zeros_ref[...] = jnp.zeros_like(zeros_ref)

        def block_copy(blk):
            return pltpu.make_async_copy(
                zeros_ref, buf_hbm.at[pl.ds(pl.multiple_of(blk * MOE_ROWS, MOE_ROWS), MOE_ROWS), :], sem_zero)

        def zero_copy(e):
            return block_copy(bstart_ref[e] + nblk_ref[e] - 1)

        def start_tail(blk, c):
            block_copy(blk).start()
            return c

        def wait_tail(blk, c):
            block_copy(blk).wait()
            return c

        lax.fori_loop(nused_ref[0], n_blocks, start_tail, 0)
        lax.fori_loop(nused_ref[0], n_blocks, wait_tail, 0)

        def start(e, c):
            @pl.when(nblk_ref[e] > 0)
            def _():
                zero_copy(e).start()
            return c

        def wait(e, c):
            @pl.when(nblk_ref[e] > 0)
            def _():
                zero_copy(e).wait()
            return c

        lax.fori_loop(0, N_EXPERTS, start, 0)
        lax.fori_loop(0, N_EXPERTS, wait, 0)

    def start_token(t, c):
        for k in range(2):
            pltpu.make_async_copy(x_ref.at[pl.ds(t, 1), :], buf_hbm.at[pl.ds(dest_ref[2 * t + k], 1), :],
                                  sem_rows).start()
        return c

    lax.fori_loop(0, td, start_token, 0, unroll=4)
    for _ in range(2):
        pltpu.make_async_copy(x_ref, buf_hbm.at[pl.ds(0, td), :], sem_rows).wait()


def _expert_kernel(blk_ref, nused_ref, x_ref, wgu_ref, wdn_ref, y_ref, wgu_bf, wdn_bf, *, d_expert):
    b = pl.program_id(0)

    @pl.when(b < nused_ref[0])
    def _():
        prev = blk_ref[jnp.maximum(b - 1, 0)]
        first = (b == 0) | (blk_ref[b] != prev)

        @pl.when(first)
        def _():
            wgu_bf[...] = wgu_ref[...].astype(BF16)
            wdn_bf[...] = wdn_ref[...].astype(BF16)

        x = x_ref[...].astype(BF16)
        gu = jnp.dot(x, wgu_bf[...], preferred_element_type=F32)
        g = gu[:, :d_expert]
        act = (g * jax.nn.sigmoid(g)) * gu[:, d_expert:]
        y_ref[...] = jnp.dot(act.astype(BF16), wdn_bf[...], preferred_element_type=F32)

    @pl.when(b >= nused_ref[0])
    def _():
        y_ref[...] = jnp.zeros_like(y_ref)


def _combine_kernel(dest_ref, dest_next_ref, y_hbm, route_ref, h_ref, g_ref, out_ref, xn_ref, rows_ref, sems,
                    *, tc, with_norm):
    step = pl.program_id(0)
    nsteps = pl.num_programs(0)
    slot = step % 2

    def issue(idx_ref, to_slot):
        def start_token(t, c):
            for k in range(2):
                pltpu.make_async_copy(y_hbm.at[pl.ds(idx_ref[2 * t + k], 1), :],
                                      rows_ref.at[to_slot, k, pl.ds(t, 1), :], sems.at[to_slot]).start()
            return c
        lax.fori_loop(0, tc, start_token, 0, unroll=4)

    @pl.when(step == 0)
    def _():
        issue(dest_ref, 0)

    @pl.when(step + 1 < nsteps)
    def _():
        issue(dest_next_ref, 1 - slot)

    for k in range(2):
        pltpu.make_async_copy(y_hbm.at[pl.ds(0, tc), :], rows_ref.at[slot, k], sems.at[slot]).wait()
    route = route_ref[...]
    out = h_ref[...] + (rows_ref[slot, 0] * route[:, 4:5] + rows_ref[slot, 1] * route[:, 5:6])
    out_ref[...] = out
    if with_norm:
        xn_ref[...] = _rms(out, g_ref[...]).astype(BF16)
    else:
        xn_ref[...] = jnp.zeros_like(xn_ref)


def _moe(xn, route, cnt, h1, w_gate_up, w_down, next_gain):
    tokens, d = xn.shape
    d_expert = w_down.shape[1]
    n_blocks = (2 * tokens + N_EXPERTS * (MOE_ROWS - 1) + MOE_ROWS - 1) // MOE_ROWS
    assert n_blocks <= 256, "block counts must stay exact in bf16"
    nb_pad = (n_blocks + 7) // 8 * 8

    plan, blk = pl.pallas_call(
        functools.partial(_plan_kernel, n_blocks=nb_pad),
        out_shape=[jax.ShapeDtypeStruct((8, LANES), F32), jax.ShapeDtypeStruct((nb_pad, LANES), F32)],
        name="moe_plan",
    )(cnt)

    tm = 512
    dest = pl.pallas_call(
        functools.partial(_dest_kernel, tm=tm),
        grid=(tokens // tm,),
        in_specs=[pl.BlockSpec((tm, LANES), lambda i: (i, 0)), pl.BlockSpec((8, LANES), lambda i: (0, 0))],
        out_specs=pl.BlockSpec((tm, LANES), lambda i: (i, 0)),
        out_shape=jax.ShapeDtypeStruct((tokens, LANES), F32),
        compiler_params=_params(("parallel",)),
        name="moe_dest",
    )(route, plan)

    dest_flat = dest[:, :2].astype(I32).reshape(-1)
    bstart = plan[0, :N_EXPERTS].astype(I32)
    nblk = plan[1, :N_EXPERTS].astype(I32)
    n_used = plan[2, :1].astype(I32)
    blk_e = blk[:n_blocks, 0].astype(I32)

    td = 512
    smem = pltpu.SMEM
    buf = pl.pallas_call(
        functools.partial(_dispatch_kernel, td=td, n_blocks=n_blocks),
        grid=(tokens // td,),
        in_specs=[
            pl.BlockSpec((2 * td,), lambda i: (i,), memory_space=smem),
            pl.BlockSpec((N_EXPERTS,), lambda i: (0,), memory_space=smem),
            pl.BlockSpec((N_EXPERTS,), lambda i: (0,), memory_space=smem),
            pl.BlockSpec((1,), lambda i: (0,), memory_space=smem),
            pl.BlockSpec((td, d), lambda i: (i, 0)),
        ],
        out_specs=pl.BlockSpec(memory_space=pl.ANY),
        out_shape=jax.ShapeDtypeStruct((n_blocks * MOE_ROWS, d), F32),
        scratch_shapes=[pltpu.VMEM((MOE_ROWS, d), F32), pltpu.SemaphoreType.DMA(()), pltpu.SemaphoreType.DMA(())],
        compiler_params=_params(("arbitrary",)),
        name="moe_dispatch",
    )(dest_flat, bstart, nblk, n_used, xn)

    y = pl.pallas_call(
        functools.partial(_expert_kernel, d_expert=d_expert),
        grid_spec=pltpu.PrefetchScalarGridSpec(
            num_scalar_prefetch=2,
            grid=(n_blocks,),
            in_specs=[
                pl.BlockSpec((MOE_ROWS, d), lambda b, be, nu: (b, 0)),
                pl.BlockSpec((None, d, 2 * d_expert), lambda b, be, nu: (be[b], 0, 0)),
                pl.BlockSpec((None, d_expert, d), lambda b, be, nu: (be[b], 0, 0)),
            ],
            out_specs=pl.BlockSpec((MOE_ROWS, d), lambda b, be, nu: (b, 0)),
            scratch_shapes=[pltpu.VMEM((d, 2 * d_expert), BF16), pltpu.VMEM((d_expert, d), BF16)],
        ),
        out_shape=jax.ShapeDtypeStruct((n_blocks * MOE_ROWS, d), F32),
        compiler_params=_params(("arbitrary",)),
        name="moe_experts",
    )(blk_e, n_used, buf, w_gate_up, w_down)

    tc = 256
    with_norm = next_gain is not None
    gain = next_gain if with_norm else jnp.ones((1, d), F32)
    row_spec = pl.BlockSpec((tc, d), lambda i: (i, 0))
    out, xn_next = pl.pallas_call(
        functools.partial(_combine_kernel, tc=tc, with_norm=with_norm),
        grid=(tokens // tc,),
        in_specs=[
            pl.BlockSpec((2 * tc,), lambda i: (i,), memory_space=smem),
            pl.BlockSpec((2 * tc,), lambda i: (jnp.minimum(i + 1, tokens // tc - 1),), memory_space=smem),
            pl.BlockSpec(memory_space=pl.ANY),
            pl.BlockSpec((tc, LANES), lambda i: (i, 0)),
            row_spec,
            pl.BlockSpec((1, d), lambda i: (0, 0)),
        ],
        out_specs=[row_spec, row_spec if with_norm else pl.BlockSpec((8, LANES), lambda i: (0, 0))],
        out_shape=[jax.ShapeDtypeStruct((tokens, d), F32),
                   jax.ShapeDtypeStruct((tokens, d), BF16) if with_norm else jax.ShapeDtypeStruct((8, LANES), BF16)],
        scratch_shapes=[pltpu.VMEM((2, 2, tc, d), F32), pltpu.SemaphoreType.DMA((2,))],
        compiler_params=_params(("arbitrary",)),
        name="moe_combine",
    )(dest_flat, dest_flat, y, route, h1, gain)
    return out, (xn_next if with_norm else None)


def _router_weights(w_group, b_group, w_expert, b_expert):
    d = w_group.shape[0]
    pad = LANES - N_EXPERTS - N_GROUPS
    w = jnp.concatenate([w_expert, w_group, jnp.zeros((d, pad), F32)], axis=1)
    b = jnp.concatenate([b_expert, b_group, jnp.zeros((pad,), F32)])[None, :]
    return w, b


def _row(v):
    return v.astype(F32)[None, :]


def kernel(x, positions, l0_norm_mix, l0_mla_w_in, l0_mla_q_lat_norm, l0_mla_kv_lat_norm, l0_mla_w_uq, l0_mla_w_ukv, l0_mla_qk_gain, l0_mla_w_o, l0_norm_ffn, l0_router_group, l0_router_group_bias, l0_router_expert, l0_router_expert_bias, l0_w_gate_up, l0_w_down, l1_norm_mix, l1_fox_w_in, l1_fox_forget_bias, l1_fox_qk_gain, l1_fox_w_o, l1_norm_ffn, l1_router_group, l1_router_group_bias, l1_router_expert, l1_router_expert_bias, l1_w_gate_up, l1_w_down):
    batch, seq, d = x.shape
    tokens = batch * seq
    q_lora = l0_mla_q_lat_norm.shape[0]
    kv_lora = l0_mla_kv_lat_norm.shape[0]
    rope = l0_mla_w_in.shape[1] - q_lora - kv_lora
    nope = l0_mla_qk_gain.shape[1] - rope
    heads = l0_mla_w_uq.shape[1] // (nope + rope)
    assert rope == 64 and nope == LANES and heads % 2 == 0
    tk_attn = min(512, seq // 2)

    h0 = x.reshape(tokens, d)
    pos = positions.reshape(tokens, 1)

    half = rope // 2
    freqs = ROPE_THETA ** (-jnp.arange(half, dtype=F32) / half)
    freq_row = jnp.tile(freqs, LANES // half)[None, :]
    gkr = _row(l0_mla_qk_gain[1, nope:])
    tm = 512
    row = lambda w: pl.BlockSpec((tm, w), lambda i: (i, 0))
    const = lambda shape: pl.BlockSpec(shape, lambda i: (0, 0))
    win = l0_mla_w_in.astype(BF16)
    cq, ckv, kr2, cos_t, sin_t = pl.pallas_call(
        functools.partial(_mla_in_kernel, q_lora=q_lora, kv_lora=kv_lora),
        grid=(tokens // tm,),
        in_specs=[row(d), row(1), const((1, d)), const(win.shape), const((1, q_lora)), const((1, kv_lora)),
                  const((1, rope)), const((1, LANES))],
        out_specs=[row(q_lora), row(kv_lora), row(LANES), row(LANES), row(LANES)],
        out_shape=[jax.ShapeDtypeStruct((tokens, q_lora), BF16), jax.ShapeDtypeStruct((tokens, kv_lora), BF16),
                   jax.ShapeDtypeStruct((tokens, LANES), BF16), jax.ShapeDtypeStruct((tokens, LANES), F32),
                   jax.ShapeDtypeStruct((tokens, LANES), F32)],
        compiler_params=_params(("parallel",)),
        name="mla_in",
    )(h0, pos, _row(l0_norm_mix), win, _row(l0_mla_q_lat_norm), _row(l0_mla_kv_lat_norm), gkr, freq_row)

    wuq = l0_mla_w_uq.reshape(q_lora, heads, nope + rope)
    wuq = jnp.concatenate([wuq[:, :, :nope].reshape(q_lora, heads * nope),
                           wuq[:, :, nope:].reshape(q_lora, heads * rope)], axis=1).astype(BF16)
    scale = LOG2E / math.sqrt(nope + rope)
    q_aug = pl.pallas_call(
        functools.partial(_mla_q_kernel, heads=heads, scale=scale),
        grid=(tokens // tm,),
        in_specs=[row(q_lora), const(wuq.shape), const((1, nope)), const((1, LANES)), row(LANES), row(LANES)],
        out_specs=row(heads * HEAD_PAD),
        out_shape=jax.ShapeDtypeStruct((tokens, heads * HEAD_PAD), BF16),
        compiler_params=_params(("parallel",)),
        name="mla_q",
    )(cq, wuq, _row(l0_mla_qk_gain[0, :nope]), _row(jnp.tile(l0_mla_qk_gain[0, nope:], 2)), cos_t, sin_t)

    v_dim = l0_mla_w_ukv.shape[1] // heads - nope
    assert v_dim == LANES
    wukv = l0_mla_w_ukv.reshape(kv_lora, heads, nope + v_dim)
    wukv = jnp.concatenate([wukv[:, :, :nope].reshape(kv_lora, heads * nope),
                            wukv[:, :, nope:].reshape(kv_lora, heads * v_dim)], axis=1).astype(BF16)
    k_aug, v0 = pl.pallas_call(
        functools.partial(_mla_kv_kernel, heads=heads),
        grid=(tokens // tm,),
        in_specs=[row(kv_lora), const(wukv.shape), const((1, nope)), row(LANES)],
        out_specs=[row(heads * HEAD_PAD), row(heads * v_dim)],
        out_shape=[jax.ShapeDtypeStruct((tokens, heads * HEAD_PAD), BF16),
                   jax.ShapeDtypeStruct((tokens, heads * v_dim), BF16)],
        compiler_params=_params(("parallel",)),
        name="mla_kv",
    )(ckv, wukv, _row(l0_mla_qk_gain[1, :nope]), kr2)

    o0 = _attention(q_aug, k_aug, v0, batch=batch, seq=seq, heads=heads, tk=tk_attn)

    wr0, br0 = _router_weights(l0_router_group, l0_router_group_bias, l0_router_expert, l0_router_expert_bias)
    h1, xn1, route0, cnt0 = _oproj_router(o0, None, h0, l0_mla_w_o.astype(BF16), _row(l0_norm_ffn), wr0, br0, tm=256)
    h2, xn2 = _moe(xn1, route0, cnt0, h1, l0_w_gate_up, l0_w_down, _row(l1_norm_mix))

    fheads = l1_fox_forget_bias.shape[0]
    hd = l1_fox_qk_gain.shape[1]
    assert hd == LANES
    hh = fheads * hd
    w1 = l1_fox_w_in
    wq1, wk1, wv1 = (w1[:, i * hh:(i + 1) * hh].astype(BF16) for i in range(3))
    wf1 = jnp.concatenate([w1[:, 3 * hh:3 * hh + fheads], jnp.zeros((d, LANES - fheads), F32)], axis=1).astype(BF16)
    wg1 = w1[:, 3 * hh + fheads:].astype(BF16)
    fbias = jnp.concatenate([l1_fox_forget_bias.astype(F32), jnp.zeros((LANES - fheads,), F32)])[None, :]

    ts = min(512, seq)
    ns = seq // ts
    cdec = pl.pallas_call(
        functools.partial(_fox_decay_kernel, ts=ts),
        grid=(batch, ns),
        in_specs=[pl.BlockSpec((ts, d), lambda b, i: (b * ns + i, 0)), pl.BlockSpec((d, LANES), lambda b, i: (0, 0)),
                  pl.BlockSpec((1, LANES), lambda b, i: (0, 0))],
        out_specs=pl.BlockSpec((ts, LANES), lambda b, i: (b * ns + i, 0)),
        out_shape=jax.ShapeDtypeStruct((tokens, LANES), F32),
        scratch_shapes=[pltpu.VMEM((1, LANES), F32)],
        compiler_params=_params(("arbitrary", "arbitrary")),
        name="fox_decay",
    )(xn2, wf1, fbias)

    def fox_proj(w, gain, mode, out_width, out_dtype):
        return pl.pallas_call(
            functools.partial(_fox_proj_kernel, heads=fheads, mode=mode,
                              scale=(LOG2E / math.sqrt(hd)) if mode == "q" else 1.0),
            grid=(tokens // tm,),
            in_specs=[row(d), const(w.shape), const((1, hd)), row(LANES)],
            out_specs=row(out_width),
            out_shape=jax.ShapeDtypeStruct((tokens, out_width), out_dtype),
            compiler_params=_params(("parallel",)),
            name="fox_proj_" + mode,
        )(xn2, w, gain, cdec)

    q1 = fox_proj(wq1, _row(l1_fox_qk_gain[0]), "q", fheads * HEAD_PAD, BF16)
    k1 = fox_proj(wk1, _row(l1_fox_qk_gain[1]), "k", fheads * HEAD_PAD, BF16)
    v1 = fox_proj(wv1, _row(l1_fox_qk_gain[0]), "v", hh, BF16)
    gate1 = fox_proj(wg1, _row(l1_fox_qk_gain[0]), "gate", hh, F32)

    o1 = _attention(q1, k1, v1, batch=batch, seq=seq, heads=fheads, tk=tk_attn)

    wr1, br1 = _router_weights(l1_router_group, l1_router_group_bias, l1_router_expert, l1_router_expert_bias)
    h3, xn3, route1, cnt1 = _oproj_router(o1, gate1, h2, l1_fox_w_o.astype(BF16), _row(l1_norm_ffn), wr1, br1, tm=256)
    h4, _ = _moe(xn3, route1, cnt1, h3, l1_w_gate_up, l1_w_down, None)
    return h4.reshape(batch, seq, d)
```

```python
import functools
import math

import jax
import jax.numpy as jnp
from jax import lax
from jax.experimental import pallas as pl
from jax.experimental.pallas import tpu as pltpu

F32 = jnp.float32
BF16 = jnp.bfloat16
I32 = jnp.int32

EPS = 1e-6
ROPE_THETA = 10000.0
LANES = 128
HEAD_PAD = 256
N_GROUPS = 8
EXPERTS_PER_GROUP = 8
N_EXPERTS = N_GROUPS * EXPERTS_PER_GROUP
MOE_ROWS = 256
NEG_BIG = -1e30
LOG2E = math.log2(math.e)
VMEM_LIMIT = 52 * 1024 * 1024


def _params(sem, vmem=VMEM_LIMIT):
    return pltpu.CompilerParams(dimension_semantics=sem, vmem_limit_bytes=vmem)


def _rms(x, gain):
    ms = jnp.mean(x * x, axis=-1, keepdims=True)
    return x * lax.rsqrt(ms + EPS) * gain


def _lane_iota(shape):
    return lax.broadcasted_iota(I32, shape, len(shape) - 1)


def _rope_slab(x, cos, sin_signed):
    lane = _lane_iota((1, LANES))
    low = (lane % 64) < 32
    swapped = jnp.where(low, pltpu.roll(x, 96, 1), pltpu.roll(x, 32, 1))
    return x * cos + swapped * sin_signed


def _split3(x):
    a = x.astype(BF16)
    r = x - a.astype(F32)
    b = r.astype(BF16)
    c = (r - b.astype(F32)).astype(BF16)
    return a, b, c


def _mla_in_kernel(h_ref, pos_ref, gmix_ref, win_ref, gq_ref, gkv_ref, gkr_ref, freq_ref,
                   cq_ref, ckv_ref, kr_ref, cos_ref, sin_ref, *, q_lora, kv_lora):
    xn = _rms(h_ref[...], gmix_ref[...]).astype(BF16)
    z = jnp.dot(xn, win_ref[...], preferred_element_type=F32)
    cq_ref[...] = _rms(z[:, :q_lora], gq_ref[...]).astype(BF16)
    ckv_ref[...] = _rms(z[:, q_lora:q_lora + kv_lora], gkv_ref[...]).astype(BF16)
    kr = _rms(z[:, q_lora + kv_lora:], gkr_ref[...])
    slab = jnp.concatenate([kr, kr], axis=-1)
    ang = pos_ref[...].astype(F32) * freq_ref[...]
    lane = _lane_iota((1, LANES))
    cos = jnp.cos(ang)
    sin_signed = jnp.where((lane % 64) < 32, -jnp.sin(ang), jnp.sin(ang))
    cos_ref[...] = cos
    sin_ref[...] = sin_signed
    kr_ref[...] = _rope_slab(slab, cos, sin_signed).astype(BF16)


def _mla_q_kernel(cq_ref, w_ref, gn_ref, gr_ref, cos_ref, sin_ref, q_ref, *, heads, scale):
    q = jnp.dot(cq_ref[...], w_ref[...], preferred_element_type=F32)
    lane = _lane_iota((1, LANES))
    low = lane < 64
    cos = cos_ref[...]
    sin_signed = sin_ref[...]
    for h in range(heads):
        n = _rms(q[:, h * LANES:(h + 1) * LANES], gn_ref[...]) * scale
        q_ref[:, h * HEAD_PAD:h * HEAD_PAD + LANES] = n.astype(BF16)
    base = heads * LANES
    for p in range(heads // 2):
        slab = q[:, base + p * LANES:base + (p + 1) * LANES]
        sq = slab * slab
        ss0 = jnp.sum(jnp.where(low, sq, 0.0), axis=-1, keepdims=True)
        ss1 = jnp.sum(jnp.where(low, 0.0, sq), axis=-1, keepdims=True)
        inv = jnp.where(low, lax.rsqrt(ss0 / 64.0 + EPS), lax.rsqrt(ss1 / 64.0 + EPS))
        r = _rope_slab(slab * inv * gr_ref[...], cos, sin_signed) * scale
        zero = jnp.zeros_like(r)
        q_ref[:, (2 * p) * HEAD_PAD + LANES:(2 * p + 1) * HEAD_PAD] = jnp.where(low, r, zero).astype(BF16)
        q_ref[:, (2 * p + 1) * HEAD_PAD + LANES:(2 * p + 2) * HEAD_PAD] = jnp.where(low, zero, r).astype(BF16)


def _mla_kv_kernel(ckv_ref, w_ref, gn_ref, kr_ref, k_ref, v_ref, *, heads):
    kv = jnp.dot(ckv_ref[...], w_ref[...], preferred_element_type=F32)
    lane = _lane_iota((1, LANES))
    low = lane < 64
    kr = kr_ref[...]
    zero = jnp.zeros_like(kr)
    kr_even = jnp.where(low, kr, zero)
    kr_odd = jnp.where(low, zero, kr)
    for h in range(heads):
        n = _rms(kv[:, h * LANES:(h + 1) * LANES], gn_ref[...])
        k_ref[:, h * HEAD_PAD:h * HEAD_PAD + LANES] = n.astype(BF16)
        k_ref[:, h * HEAD_PAD + LANES:(h + 1) * HEAD_PAD] = kr_even if h % 2 == 0 else kr_odd
    v_ref[...] = kv[:, heads * LANES:].astype(BF16)


def _fox_decay_kernel(xn_ref, w_ref, b_ref, c_ref, carry_ref, *, ts):
    @pl.when(pl.program_id(1) == 0)
    def _():
        carry_ref[...] = jnp.zeros_like(carry_ref)

    f = jnp.dot(xn_ref[...], w_ref[...], preferred_element_type=F32) + b_ref[...]
    log_f = jnp.minimum(f, 0.0) - jnp.log1p(jnp.exp(-jnp.abs(f)))
    row = lax.broadcasted_iota(I32, (ts, ts), 0)
    col = lax.broadcasted_iota(I32, (ts, ts), 1)
    tri = jnp.where(col <= row, 1.0, 0.0).astype(BF16)
    a, b, c = _split3(log_f)
    parts = jnp.dot(tri, jnp.concatenate([a, b, c], axis=-1), preferred_element_type=F32)
    local = parts[:, :LANES] + parts[:, LANES:2 * LANES] + parts[:, 2 * LANES:]
    out = carry_ref[...] + local
    c_ref[...] = out
    carry_ref[...] = out[ts - 1:ts, :]


def _fox_proj_kernel(xn_ref, w_ref, g_ref, c_ref, o_ref, *, heads, mode, scale):
    z = jnp.dot(xn_ref[...], w_ref[...], preferred_element_type=F32)
    if mode == "v":
        o_ref[...] = z.astype(BF16)
        return
    if mode == "gate":
        o_ref[...] = z
        return
    lane = _lane_iota((1, LANES))
    cdec = c_ref[...] * LOG2E
    for h in range(heads):
        n = _rms(z[:, h * LANES:(h + 1) * LANES], g_ref[...]) * scale
        o_ref[:, h * HEAD_PAD:h * HEAD_PAD + LANES] = n.astype(BF16)
        ch = cdec[:, h:h + 1]
        if mode == "k":
            ch = -ch
        a, b, c = (t.astype(F32) for t in _split3(ch))
        if mode == "q":
            aug = jnp.where(lane == 0, a, jnp.where(lane == 1, b, jnp.where(lane == 2, c,
                  jnp.where(lane < 6, 1.0, 0.0))))
        else:
            aug = jnp.where(lane < 3, 1.0, jnp.where(lane == 3, a, jnp.where(lane == 4, b,
                  jnp.where(lane == 5, c, 0.0))))
        o_ref[:, h * HEAD_PAD + LANES:(h + 1) * HEAD_PAD] = aug.astype(BF16)


def _attn_tile(q, k_ref, v_ref, j, m_ref, l_ref, acc_ref, *, tk, masked):
    rows = q.shape[0]
    start = pl.multiple_of(j * tk, tk)
    k = k_ref[pl.ds(start, tk), :]
    v = v_ref[pl.ds(start, tk), :]
    s = lax.dot_general(q, k, (((1,), (1,)), ((), ())), preferred_element_type=F32)
    if masked:
        row = lax.broadcasted_iota(I32, (rows, tk), 0)
        col = lax.broadcasted_iota(I32, (rows, tk), 1)
        s = jnp.where(row >= col, s, NEG_BIG)
    m_prev = m_ref[...]
    m_new = jnp.maximum(m_prev, jnp.max(s, axis=-1, keepdims=True))
    alpha = jnp.exp2(m_prev - m_new)
    p = jnp.exp2(s - jnp.concatenate([m_new] * (tk // LANES), axis=1))
    psum = p[:, :LANES]
    for c in range(1, tk // LANES):
        psum = psum + p[:, c * LANES:(c + 1) * LANES]
    l_ref[...] = alpha * l_ref[...] + psum
    acc_ref[...] = alpha * acc_ref[...] + jnp.dot(p.astype(BF16), v, preferred_element_type=F32)
    m_ref[...] = m_new


def _attn_kernel(q_ref, k_ref, v_ref, o_ref, m_a, l_a, acc_a, m_b, l_b, acc_b, *, tk):
    qi = pl.program_id(2)
    for m, l, acc in ((m_a, l_a, acc_a), (m_b, l_b, acc_b)):
        m[...] = jnp.full_like(m, NEG_BIG)
        l[...] = jnp.zeros_like(l)
        acc[...] = jnp.zeros_like(acc)
    q = q_ref[...]
    tile = functools.partial(_attn_tile, k_ref=k_ref, v_ref=v_ref, tk=tk)

    def body(jj, carry):
        tile(q, j=2 * jj, m_ref=m_a, l_ref=l_a, acc_ref=acc_a, masked=False)
        tile(q, j=2 * jj + 1, m_ref=m_b, l_ref=l_b, acc_ref=acc_b, masked=False)
        return carry

    lax.fori_loop(0, qi, body, 0)
    tile(q, j=2 * qi, m_ref=m_a, l_ref=l_a, acc_ref=acc_a, masked=True)
    low = pl.ds(tk, tk)
    tile(q_ref[low, :], j=2 * qi + 1, m_ref=m_b.at[low, :], l_ref=l_b.at[low, :], acc_ref=acc_b.at[low, :],
         masked=True)
    m = jnp.maximum(m_a[...], m_b[...])
    w_a = jnp.exp2(m_a[...] - m)
    w_b = jnp.exp2(m_b[...] - m)
    l = jnp.sum(w_a * l_a[...] + w_b * l_b[...], axis=-1, keepdims=True)
    o_ref[...] = ((w_a * acc_a[...] + w_b * acc_b[...]) / l).astype(BF16)


def _attention(q, k, v, *, batch, seq, heads, tk):
    tq = 2 * tk
    nq = seq // tq
    stat = pltpu.VMEM((tq, LANES), F32)
    return pl.pallas_call(
        functools.partial(_attn_kernel, tk=tk),
        grid=(batch, heads, nq),
        in_specs=[
            pl.BlockSpec((tq, HEAD_PAD), lambda b, h, i: (b * nq + i, h)),
            pl.BlockSpec((seq, HEAD_PAD), lambda b, h, i: (b, h)),
            pl.BlockSpec((seq, LANES), lambda b, h, i: (b, h)),
        ],
        out_specs=pl.BlockSpec((tq, LANES), lambda b, h, i: (b * nq + i, h)),
        out_shape=jax.ShapeDtypeStruct((batch * seq, heads * LANES), BF16),
        scratch_shapes=[stat] * 6,
        compiler_params=_params(("parallel", "parallel", "arbitrary")),
        name="causal_attention",
    )(q, k, v)


def _oproj_router_kernel(*refs, tm, gated):
    if gated:
        (o_ref, gate_ref, h_ref, wo_ref, gffn_ref, wr_ref, br_ref,
         h1_ref, xn_ref, route_ref, cnt_ref, carry_ref, wsplit_ref) = refs
    else:
        (o_ref, h_ref, wo_ref, gffn_ref, wr_ref, br_ref,
         h1_ref, xn_ref, route_ref, cnt_ref, carry_ref, wsplit_ref) = refs

    @pl.when(pl.program_id(0) == 0)
    def _():
        carry_ref[...] = jnp.zeros_like(carry_ref)
        w = wr_ref[...]
        w_hi = w.astype(BF16)
        wsplit_ref[:, :LANES] = w_hi
        wsplit_ref[:, LANES:] = (w - w_hi.astype(F32)).astype(BF16)

    o = o_ref[...]
    if gated:
        o = (o.astype(F32) * jax.nn.sigmoid(gate_ref[...])).astype(BF16)
    h1 = h_ref[...] + jnp.dot(o, wo_ref[...], preferred_element_type=F32)
    h1_ref[...] = h1
    xn = _rms(h1, gffn_ref[...])
    xn_ref[...] = xn

    x_hi = xn.astype(BF16)
    x_lo = (xn - x_hi.astype(F32)).astype(BF16)
    part = jnp.dot(x_hi, wsplit_ref[...], preferred_element_type=F32)
    logits = (part[:, :LANES] + part[:, LANES:]
              + jnp.dot(x_lo, wsplit_ref[:, :LANES], preferred_element_type=F32)) + br_ref[...]

    lane_i = _lane_iota((tm, LANES))
    lane = lane_i.astype(F32)
    big = 1e6
    is_g = (lane_i >= N_EXPERTS) & (lane_i < N_EXPERTS + N_GROUPS)
    gl = jnp.where(is_g, logits, NEG_BIG)
    gmax = jnp.max(gl, axis=-1, keepdims=True)
    grp = jnp.min(jnp.where(gl == gmax, lane, big), axis=-1, keepdims=True) - N_EXPERTS
    p_g = 1.0 / jnp.sum(jnp.where(is_g, jnp.exp(gl - gmax), 0.0), axis=-1, keepdims=True)

    in_grp = (lane_i // EXPERTS_PER_GROUP).astype(F32) == grp
    el = jnp.where(in_grp, logits, NEG_BIG)
    emax = jnp.max(el, axis=-1, keepdims=True)
    pe = jnp.where(in_grp, jnp.exp(el - emax), 0.0)
    pe = pe / jnp.sum(pe, axis=-1, keepdims=True)
    pm = jnp.where(in_grp, pe, -1.0)
    p1 = jnp.max(pm, axis=-1, keepdims=True)
    i1 = jnp.min(jnp.where(pm == p1, lane, big), axis=-1, keepdims=True)
    pm2 = jnp.where(lane == i1, -1.0, pm)
    p2 = jnp.max(pm2, axis=-1, keepdims=True)
    i2 = jnp.min(jnp.where(pm2 == p2, lane, big), axis=-1, keepdims=True)
    den = p1 + p2
    g1 = p_g * p1 / den
    g2 = p_g * p2 / den

    hit1 = lane == i1
    hit2 = lane == i2
    oh1 = jnp.where(hit1, 1.0, 0.0)
    oh2 = jnp.where(hit2, 1.0, 0.0)
    row = lax.broadcasted_iota(I32, (tm, tm), 0)
    col = lax.broadcasted_iota(I32, (tm, tm), 1)
    tri = jnp.where(col < row, 1.0, 0.0).astype(BF16)
    cum1 = jnp.dot(tri, oh1.astype(BF16), preferred_element_type=F32)
    cum2 = jnp.dot(tri, oh2.astype(BF16), preferred_element_type=F32)
    tot1 = jnp.sum(oh1, axis=0, keepdims=True)
    tot2 = jnp.sum(oh2, axis=0, keepdims=True)
    carry = carry_ref[...]
    r1 = jnp.sum(jnp.where(hit1, carry + cum1, 0.0), axis=-1, keepdims=True)
    r2 = jnp.sum(jnp.where(hit2, carry + tot1 + cum2, 0.0), axis=-1, keepdims=True)
    new_carry = carry + tot1 + tot2
    carry_ref[...] = new_carry
    cnt_ref[...] = new_carry

    zero = jnp.zeros((tm, LANES), F32)
    route = jnp.where(lane_i == 0, i1, jnp.where(lane_i == 1, i2,
            jnp.where(lane_i == 2, r1, jnp.where(lane_i == 3, r2,
            jnp.where(lane_i == 4, g1, jnp.where(lane_i == 5, g2, zero))))))
    route_ref[...] = route


def _oproj_router(o, gate, h, wo, gffn, wr, br, *, tm):
    tokens, d = h.shape
    gated = gate is not None
    row_spec = pl.BlockSpec((tm, d), lambda i: (i, 0))
    const = lambda shape: pl.BlockSpec(shape, lambda i: (0, 0))
    in_specs = [row_spec] + ([row_spec] if gated else []) + [
        row_spec, const(wo.shape), const((1, d)), const(wr.shape), const((1, LANES))]
    args = [o] + ([gate] if gated else []) + [h, wo, gffn, wr, br]
    return pl.pallas_call(
        functools.partial(_oproj_router_kernel, tm=tm, gated=gated),
        grid=(tokens // tm,),
        in_specs=in_specs,
        out_specs=[row_spec, row_spec, pl.BlockSpec((tm, LANES), lambda i: (i, 0)), const((1, LANES))],
        out_shape=[jax.ShapeDtypeStruct((tokens, d), F32), jax.ShapeDtypeStruct((tokens, d), F32),
                   jax.ShapeDtypeStruct((tokens, LANES), F32), jax.ShapeDtypeStruct((1, LANES), F32)],
        scratch_shapes=[pltpu.VMEM((1, LANES), F32), pltpu.VMEM((d, 2 * LANES), BF16)],
        compiler_params=_params(("arbitrary",)),
        name="oproj_router",
    )(*args)


def _plan_kernel(cnt_ref, plan_ref, blk_ref, *, n_blocks):
    cnt = cnt_ref[...]
    nblk = jnp.floor((cnt + (MOE_ROWS - 1)) * (1.0 / MOE_ROWS))
    row = lax.broadcasted_iota(I32, (LANES, LANES), 0)
    col = lax.broadcasted_iota(I32, (LANES, LANES), 1)
    upper = jnp.where(row < col, 1.0, 0.0).astype(BF16)
    nb8 = jnp.broadcast_to(nblk, (8, LANES)).astype(BF16)
    bstart = jnp.dot(nb8, upper, preferred_element_type=F32)[0:1, :]
    bend = bstart + nblk
    lane = _lane_iota((1, LANES))
    total = jnp.sum(nblk, axis=-1, keepdims=True)
    sub = lax.broadcasted_iota(I32, (8, LANES), 0)
    plan = jnp.where(sub == 0, jnp.broadcast_to(bstart, (8, LANES)),
           jnp.where(sub == 1, jnp.broadcast_to(nblk, (8, LANES)),
           jnp.where(sub == 2, jnp.broadcast_to(total, (8, LANES)), 0.0)))
    plan_ref[...] = plan
    bidx = lax.broadcasted_iota(I32, (n_blocks, LANES), 0).astype(F32)
    done = jnp.where((_lane_iota((n_blocks, LANES)) < N_EXPERTS) & (bend <= bidx), 1.0, 0.0)
    last_used = jnp.max(jnp.where(nblk > 0.0, lane.astype(F32), 0.0), axis=-1, keepdims=True)
    e_of_b = jnp.minimum(jnp.sum(done, axis=-1, keepdims=True), last_used)
    blk_ref[...] = jnp.broadcast_to(e_of_b, (n_blocks, LANES))


def _dest_kernel(route_ref, plan_ref, dest_ref, *, tm):
    route = route_ref[...]
    bstart = plan_ref[0:1, :] * float(MOE_ROWS)
    lane = _lane_iota((tm, LANES)).astype(F32)
    e1 = route[:, 0:1]
    e2 = route[:, 1:2]
    d1 =jnp.sum(jnp.where(lane == e1, bstart, 0.0), axis=-1, keepdims=True) + route[:, 2:3]
    d2 = jnp.sum(jnp.where(lane == e2, bstart, 0.0), axis=-1, keepdims=True) + route[:, 3:4]
    dest_ref[...] = jnp.where(lane == 0, d1, jnp.where(lane == 1, d2, 0.0))


def _dispatch_kernel(dest_ref, bstart_ref, nblk_ref, nused_ref, x_ref, buf_hbm, zeros_ref, sem_zero, sem_rows,
                     *, td, n_blocks):
    step = pl.program_id(0)

    @pl.when(step == 0)
    def _():
        zeros_ref[...] = jnp.zeros_like(zeros_ref)

        def block_copy(blk):
            return pltpu.make_async_copy(
                zeros_ref, buf_hbm.at[pl.ds(pl.multiple_of(blk * MOE_ROWS, MOE_ROWS), MOE_ROWS), :], sem_zero)

        def zero_copy(e):
            return block_copy(bstart_ref[e] + nblk_ref[e] - 1)

        def start_tail(blk, c):
            block_copy(blk).start()
            return c

        def wait_tail(blk, c):
            block_copy(blk).wait()
            return c

        lax.fori_loop(nused_ref[0], n_blocks, start_tail, 0)
        lax.fori_loop(nused_ref[0], n_blocks, wait_tail, 0)

        def start(e, c):
            @pl.when(nblk_ref[e] > 0)
            def _():
                zero_copy(e).start()
            return c

        def wait(e, c):
            @pl.when(nblk_ref[e] > 0)
            def _():
                zero_copy(e).wait()
            return c

        lax.fori_loop(0, N_EXPERTS, start, 0)
        lax.fori_loop(0, N_EXPERTS, wait, 0)

    def start_token(t, c):
        for k in range(2):
            pltpu.make_async_copy(x_ref.at[pl.ds(t, 1), :], buf_hbm.at[pl.ds(dest_ref[2 * t + k], 1), :],
                                  sem_rows).start()
        return c

    lax.fori_loop(0, td, start_token, 0, unroll=4)
    for _ in range(2):
        pltpu.make_async_copy(x_ref, buf_hbm.at[pl.ds(0, td), :], sem_rows).wait()


def _expert_kernel(blk_ref, nused_ref, bstart_ref, nblk_ref, x_ref, wgu_hbm, wdn_hbm, y_ref,
                   wgu_f32, wdn_f32, wgu_bf, wdn_bf, slot_ref, sems, *, d_expert):
    b = pl.program_id(0)

    def weight_copies(e, slot):
        return (pltpu.make_async_copy(wgu_hbm.at[e], wgu_f32.at[slot], sems.at[0, slot]),
                pltpu.make_async_copy(wdn_hbm.at[e], wdn_f32.at[slot], sems.at[1, slot]))

    @pl.when(b == 0)
    def _():
        slot_ref[0] = 0
        for c in weight_copies(blk_ref[0], 0):
            c.start()

    @pl.when(b < nused_ref[0])
    def _():
        e = blk_ref[b]
        first = (b == 0) | (e != blk_ref[jnp.maximum(b - 1, 0)])

        @pl.when(first)
        def _():
            slot = slot_ref[0]
            nxt = bstart_ref[e] + nblk_ref[e]

            @pl.when(nxt < nused_ref[0])
            def _():
                for c in weight_copies(blk_ref[nxt], 1 - slot):
                    c.start()

            for c in weight_copies(e, slot):
                c.wait()
            wgu_bf[...] = wgu_f32[slot].astype(BF16)
            wdn_bf[...] = wdn_f32[slot].astype(BF16)
            slot_ref[0] = 1 - slot

        x = x_ref[...].astype(BF16)
        gu = jnp.dot(x, wgu_bf[...], preferred_element_type=F32)
        g = gu[:, :d_expert]
        act = (g * jax.nn.sigmoid(g)) * gu[:, d_expert:]
        y_ref[...] = jnp.dot(act.astype(BF16), wdn_bf[...], preferred_element_type=F32)

    @pl.when(b >= nused_ref[0])
    def _():
        y_ref[...] = jnp.zeros_like(y_ref)


def _combine_kernel(dest_ref, dest_next_ref, y_hbm, route_ref, h_ref, g_ref, out_ref, xn_ref, rows_ref, sems,
                    *, tc, with_norm):
    step = pl.program_id(0)
    nsteps = pl.num_programs(0)
    slot = step % 2

    def issue(idx_ref, to_slot):
        def start_token(t, c):
            for k in range(2):
                pltpu.make_async_copy(y_hbm.at[pl.ds(idx_ref[2 * t + k], 1), :],
                                      rows_ref.at[to_slot, k, pl.ds(t, 1), :], sems.at[to_slot]).start()
            return c
        lax.fori_loop(0, tc, start_token, 0, unroll=4)

    @pl.when(step == 0)
    def _():
        issue(dest_ref, 0)

    @pl.when(step + 1 < nsteps)
    def _():
        issue(dest_next_ref, 1 - slot)

    for k in range(2):
        pltpu.make_async_copy(y_hbm.at[pl.ds(0, tc), :], rows_ref.at[slot, k], sems.at[slot]).wait()
    route = route_ref[...]
    out = h_ref[...] + (rows_ref[slot, 0] * route[:, 4:5] + rows_ref[slot, 1] * route[:, 5:6])
    out_ref[...] = out
    if with_norm:
        xn_ref[...] = _rms(out, g_ref[...]).astype(BF16)
    else:
        xn_ref[...] = jnp.zeros_like(xn_ref)


def _moe(xn, route, cnt, h1, w_gate_up, w_down, next_gain):
    tokens, d = xn.shape
    d_expert = w_down.shape[1]
    n_blocks = (2 * tokens + N_EXPERTS * (MOE_ROWS - 1) + MOE_ROWS - 1) // MOE_ROWS
    assert n_blocks <= 256, "block counts must stay exact in bf16"
    nb_pad = (n_blocks + 7) // 8 * 8

    plan, blk = pl.pallas_call(
        functools.partial(_plan_kernel, n_blocks=nb_pad),
        out_shape=[jax.ShapeDtypeStruct((8, LANES), F32), jax.ShapeDtypeStruct((nb_pad, LANES), F32)],
        name="moe_plan",
    )(cnt)

    tm = 512
    dest = pl.pallas_call(
        functools.partial(_dest_kernel, tm=tm),
        grid=(tokens // tm,),
        in_specs=[pl.BlockSpec((tm, LANES), lambda i: (i, 0)), pl.BlockSpec((8, LANES), lambda i: (0, 0))],
        out_specs=pl.BlockSpec((tm, LANES), lambda i: (i, 0)),
        out_shape=jax.ShapeDtypeStruct((tokens, LANES), F32),
        compiler_params=_params(("parallel",)),
        name="moe_dest",
    )(route, plan)

    dest_flat = dest[:, :2].astype(I32).reshape(-1)
    bstart = plan[0, :N_EXPERTS].astype(I32)
    nblk = plan[1, :N_EXPERTS].astype(I32)
    n_used = plan[2, :1].astype(I32)
    blk_e = blk[:n_blocks, 0].astype(I32)

    td = 512
    smem = pltpu.SMEM
    buf = pl.pallas_call(
        functools.partial(_dispatch_kernel, td=td, n_blocks=n_blocks),
        grid=(tokens // td,),
        in_specs=[
            pl.BlockSpec((2 * td,), lambda i: (i,), memory_space=smem),
            pl.BlockSpec((N_EXPERTS,), lambda i: (0,), memory_space=smem),
            pl.BlockSpec((N_EXPERTS,), lambda i: (0,), memory_space=smem),
            pl.BlockSpec((1,), lambda i: (0,), memory_space=smem),
            pl.BlockSpec((td, d), lambda i: (i, 0)),
        ],
        out_specs=pl.BlockSpec(memory_space=pl.ANY),
        out_shape=jax.ShapeDtypeStruct((n_blocks * MOE_ROWS, d), F32),
        scratch_shapes=[pltpu.VMEM((MOE_ROWS, d), F32), pltpu.SemaphoreType.DMA(()), pltpu.SemaphoreType.DMA(())],
        compiler_params=_params(("arbitrary",)),
        name="moe_dispatch",
    )(dest_flat, bstart, nblk, n_used, xn)

    y = pl.pallas_call(
        functools.partial(_expert_kernel, d_expert=d_expert),
        grid_spec=pltpu.PrefetchScalarGridSpec(
            num_scalar_prefetch=4,
            grid=(n_blocks,),
            in_specs=[
                pl.BlockSpec((MOE_ROWS, d), lambda b, *_: (b, 0)),
                pl.BlockSpec(memory_space=pl.ANY),
                pl.BlockSpec(memory_space=pl.ANY),
            ],
            out_specs=pl.BlockSpec((MOE_ROWS, d), lambda b, *_: (b, 0)),
            scratch_shapes=[pltpu.VMEM((2, d, 2 * d_expert), F32), pltpu.VMEM((2, d_expert, d), F32),
                            pltpu.VMEM((d, 2 * d_expert), BF16), pltpu.VMEM((d_expert, d), BF16),
                            pltpu.SMEM((1,), I32), pltpu.SemaphoreType.DMA((2, 2))],
        ),
        out_shape=jax.ShapeDtypeStruct((n_blocks * MOE_ROWS, d), F32),
        compiler_params=_params(("arbitrary",)),
        name="moe_experts",
    )(blk_e, n_used, bstart, nblk, buf, w_gate_up, w_down)

    tc = 256
    with_norm = next_gain is not None
    gain = next_gain if with_norm else jnp.ones((1, d), F32)
    row_spec = pl.BlockSpec((tc, d), lambda i: (i, 0))
    out, xn_next = pl.pallas_call(
        functools.partial(_combine_kernel, tc=tc, with_norm=with_norm),
        grid=(tokens // tc,),
        in_specs=[
            pl.BlockSpec((2 * tc,), lambda i: (i,), memory_space=smem),
            pl.BlockSpec((2 * tc,), lambda i: (jnp.minimum(i + 1, tokens // tc - 1),), memory_space=smem),
            pl.BlockSpec(memory_space=pl.ANY),
            pl.BlockSpec((tc, LANES), lambda i: (i, 0)),
            row_spec,
            pl.BlockSpec((1, d), lambda i: (0, 0)),
        ],
        out_specs=[row_spec, row_spec if with_norm else pl.BlockSpec((8, LANES), lambda i: (0, 0))],
        out_shape=[jax.ShapeDtypeStruct((tokens, d), F32),
                   jax.ShapeDtypeStruct((tokens, d), BF16) if with_norm else jax.ShapeDtypeStruct((8, LANES), BF16)],
        scratch_shapes=[pltpu.VMEM((2, 2, tc, d), F32), pltpu.SemaphoreType.DMA((2,))],
        compiler_params=_params(("arbitrary",)),
        name="moe_combine",
    )(dest_flat, dest_flat, y, route, h1, gain)
    return out, (xn_next if with_norm else None)


def _router_weights(w_group, b_group, w_expert, b_expert):
    d = w_group.shape[0]
    pad = LANES - N_EXPERTS - N_GROUPS
    w = jnp.concatenate([w_expert, w_group, jnp.zeros((d, pad), F32)], axis=1)
    b = jnp.concatenate([b_expert, b_group, jnp.zeros((pad,), F32)])[None, :]
    return w, b


def _row(v):
    return v.astype(F32)[None, :]


def kernel(x, positions, l0_norm_mix, l0_mla_w_in, l0_mla_q_lat_norm, l0_mla_kv_lat_norm, l0_mla_w_uq, l0_mla_w_ukv, l0_mla_qk_gain, l0_mla_w_o, l0_norm_ffn, l0_router_group, l0_router_group_bias, l0_router_expert, l0_router_expert_bias, l0_w_gate_up, l0_w_down, l1_norm_mix, l1_fox_w_in, l1_fox_forget_bias, l1_fox_qk_gain, l1_fox_w_o, l1_norm_ffn, l1_router_group, l1_router_group_bias, l1_router_expert, l1_router_expert_bias, l1_w_gate_up, l1_w_down):
    batch, seq, d = x.shape
    tokens = batch * seq
    q_lora = l0_mla_q_lat_norm.shape[0]
    kv_lora = l0_mla_kv_lat_norm.shape[0]
    rope = l0_mla_w_in.shape[1] - q_lora - kv_lora
    nope = l0_mla_qk_gain.shape[1] - rope
    heads = l0_mla_w_uq.shape[1] // (nope + rope)
    assert rope == 64 and nope == LANES and heads % 2 == 0
    tk_attn = min(512, seq // 2)

    h0 = x.reshape(tokens, d)
    pos = positions.reshape(tokens, 1)

    half = rope // 2
    freqs = ROPE_THETA ** (-jnp.arange(half, dtype=F32) / half)
    freq_row = jnp.tile(freqs, LANES // half)[None, :]
    gkr = _row(l0_mla_qk_gain[1, nope:])
    tm = 512
    row = lambda w: pl.BlockSpec((tm, w), lambda i: (i, 0))
    const = lambda shape: pl.BlockSpec(shape, lambda i: (0, 0))
    win = l0_mla_w_in.astype(BF16)
    cq, ckv, kr2, cos_t, sin_t = pl.pallas_call(
        functools.partial(_mla_in_kernel, q_lora=q_lora, kv_lora=kv_lora),
        grid=(tokens // tm,),
        in_specs=[row(d), row(1), const((1, d)), const(win.shape), const((1, q_lora)), const((1, kv_lora)),
                  const((1, rope)), const((1, LANES))],
        out_specs=[row(q_lora), row(kv_lora), row(LANES), row(LANES), row(LANES)],
        out_shape=[jax.ShapeDtypeStruct((tokens, q_lora), BF16), jax.ShapeDtypeStruct((tokens, kv_lora), BF16),
                   jax.ShapeDtypeStruct((tokens, LANES), BF16), jax.ShapeDtypeStruct((tokens, LANES), F32),
                   jax.ShapeDtypeStruct((tokens, LANES), F32)],
        compiler_params=_params(("parallel",)),
        name="mla_in",
    )(h0, pos, _row(l0_norm_mix), win, _row(l0_mla_q_lat_norm), _row(l0_mla_kv_lat_norm), gkr, freq_row)

    wuq = l0_mla_w_uq.reshape(q_lora, heads, nope + rope)
    wuq = jnp.concatenate([wuq[:, :, :nope].reshape(q_lora, heads * nope),
                           wuq[:, :, nope:].reshape(q_lora, heads * rope)], axis=1).astype(BF16)
    scale = LOG2E / math.sqrt(nope + rope)
    q_aug = pl.pallas_call(
        functools.partial(_mla_q_kernel, heads=heads, scale=scale),
        grid=(tokens // tm,),
        in_specs=[row(q_lora), const(wuq.shape), const((1, nope)), const((1, LANES)), row(LANES), row(LANES)],
        out_specs=row(heads * HEAD_PAD),
        out_shape=jax.ShapeDtypeStruct((tokens, heads * HEAD_PAD), BF16),
        compiler_params=_params(("parallel",)),
        name="mla_q",
    )(cq, wuq, _row(l0_mla_qk_gain[0, :nope]), _row(jnp.tile(l0_mla_qk_gain[0, nope:], 2)), cos_t, sin_t)

    v_dim = l0_mla_w_ukv.shape[1] // heads - nope
    assert v_dim == LANES
    wukv = l0_mla_w_ukv.reshape(kv_lora, heads, nope + v_dim)
    wukv = jnp.concatenate([wukv[:, :, :nope].reshape(kv_lora, heads * nope),
                            wukv[:, :, nope:].reshape(kv_lora, heads * v_dim)], axis=1).astype(BF16)
    k_aug, v0 = pl.pallas_call(
        functools.partial(_mla_kv_kernel, heads=heads),
        grid=(tokens // tm,),
        in_specs=[row(kv_lora), const(wukv.shape), const((1, nope)), row(LANES)],
        out_specs=[row(heads * HEAD_PAD), row(heads * v_dim)],
        out_shape=[jax.ShapeDtypeStruct((tokens, heads * HEAD_PAD), BF16),
                   jax.ShapeDtypeStruct((tokens, heads * v_dim), BF16)],
        compiler_params=_params(("parallel",)),
        name="mla_kv",
    )(ckv, wukv, _row(l0_mla_qk_gain[1, :nope]), kr2)

    o0 = _attention(q_aug, k_aug, v0, batch=batch, seq=seq, heads=heads, tk=tk_attn)

    wr0, br0 = _router_weights(l0_router_group, l0_router_group_bias, l0_router_expert, l0_router_expert_bias)
    h1, xn1, route0, cnt0 = _oproj_router(o0, None, h0, l0_mla_w_o.astype(BF16), _row(l0_norm_ffn), wr0, br0, tm=512)
    h2, xn2 = _moe(xn1, route0, cnt0, h1, l0_w_gate_up, l0_w_down, _row(l1_norm_mix))

    fheads = l1_fox_forget_bias.shape[0]
    hd = l1_fox_qk_gain.shape[1]
    assert hd == LANES
    hh = fheads * hd
    w1 = l1_fox_w_in
    wq1, wk1, wv1 = (w1[:, i * hh:(i + 1) * hh].astype(BF16) for i in range(3))
    wf1 = jnp.concatenate([w1[:, 3 * hh:3 * hh + fheads], jnp.zeros((d, LANES - fheads), F32)], axis=1).astype(BF16)
    wg1 = w1[:, 3 * hh + fheads:].astype(BF16)
    fbias = jnp.concatenate([l1_fox_forget_bias.astype(F32), jnp.zeros((LANES - fheads,), F32)])[None, :]

    ts = min(512, seq)
    ns = seq // ts
    cdec = pl.pallas_call(
        functools.partial(_fox_decay_kernel, ts=ts),
        grid=(batch, ns),
        in_specs=[pl.BlockSpec((ts, d), lambda b, i: (b * ns + i, 0)), pl.BlockSpec((d, LANES), lambda b, i: (0, 0)),
                  pl.BlockSpec((1, LANES), lambda b, i: (0, 0))],
        out_specs=pl.BlockSpec((ts, LANES), lambda b, i: (b * ns + i, 0)),
        out_shape=jax.ShapeDtypeStruct((tokens, LANES), F32),
        scratch_shapes=[pltpu.VMEM((1, LANES), F32)],
        compiler_params=_params(("arbitrary", "arbitrary")),
        name="fox_decay",
    )(xn2, wf1, fbias)

    def fox_proj(w, gain, mode, out_width, out_dtype):
        return pl.pallas_call(
            functools.partial(_fox_proj_kernel, heads=fheads, mode=mode,
                              scale=(LOG2E / math.sqrt(hd)) if mode == "q" else 1.0),
            grid=(tokens // tm,),
            in_specs=[row(d), const(w.shape), const((1, hd)), row(LANES)],
            out_specs=row(out_width),
            out_shape=jax.ShapeDtypeStruct((tokens, out_width), out_dtype),
            compiler_params=_params(("parallel",)),
            name="fox_proj_" + mode,
        )(xn2, w, gain, cdec)

    q1 = fox_proj(wq1, _row(l1_fox_qk_gain[0]), "q", fheads * HEAD_PAD, BF16)
    k1 = fox_proj(wk1, _row(l1_fox_qk_gain[1]), "k", fheads * HEAD_PAD, BF16)
    v1 = fox_proj(wv1, _row(l1_fox_qk_gain[0]), "v", hh, BF16)
    gate1 = fox_proj(wg1, _row(l1_fox_qk_gain[0]), "gate", hh, F32)

    o1 = _attention(q1, k1, v1, batch=batch, seq=seq, heads=fheads, tk=tk_attn)

    wr1, br1 = _router_weights(l1_router_group, l1_router_group_bias, l1_router_expert, l1_router_expert_bias)
    h3, xn3, route1, cnt1 = _oproj_router(o1, gate1, h2, l1_fox_w_o.astype(BF16), _row(l1_norm_ffn), wr1, br1, tm=512)
    h4, _ = _moe(xn3, route1, cnt1, h3, l1_w_gate_up, l1_w_down, None)
    return h4.reshape(batch, seq, d)
```

```python
import functools
import math

import jax
import jax.numpy as jnp
from jax import lax
from jax.experimental import pallas as pl
from jax.experimental.pallas import tpu as pltpu

F32 = jnp.float32
BF16 = jnp.bfloat16
I32 = jnp.int32

EPS = 1e-6
ROPE_THETA = 10000.0
LANES = 128
HEAD_PAD = 256
N_GROUPS = 8
EXPERTS_PER_GROUP = 8
N_EXPERTS = N_GROUPS * EXPERTS_PER_GROUP
MOE_ROWS = 256
NEG_BIG = -1e30
LOG2E = math.log2(math.e)
VMEM_LIMIT = 52 * 1024 * 1024


def _params(sem, vmem=VMEM_LIMIT):
    return pltpu.CompilerParams(dimension_semantics=sem, vmem_limit_bytes=vmem)


def _rms(x, gain):
    ms = jnp.mean(x * x, axis=-1, keepdims=True)
    return x * lax.rsqrt(ms + EPS) * gain


def _lane_iota(shape):
    return lax.broadcasted_iota(I32, shape, len(shape) - 1)


def _rope_slab(x, cos, sin_signed):
    lane = _lane_iota((1, LANES))
    low = (lane % 64) < 32
    swapped = jnp.where(low, pltpu.roll(x, 96, 1), pltpu.roll(x, 32, 1))
    return x * cos + swapped * sin_signed


def _split3(x):
    a = x.astype(BF16)
    r = x - a.astype(F32)
    b = r.astype(BF16)
    c = (r - b.astype(F32)).astype(BF16)
    return a, b, c


def _mla_in_kernel(h_ref, pos_ref, gmix_ref, win_ref, gq_ref, gkv_ref, gkr_ref, freq_ref,
                   cq_ref, ckv_ref, kr_ref, cos_ref, sin_ref, *, q_lora, kv_lora):
    xn = _rms(h_ref[...], gmix_ref[...]).astype(BF16)
    z = jnp.dot(xn, win_ref[...], preferred_element_type=F32)
    cq_ref[...] = _rms(z[:, :q_lora], gq_ref[...]).astype(BF16)
    ckv_ref[...] = _rms(z[:, q_lora:q_lora + kv_lora], gkv_ref[...]).astype(BF16)
    kr = _rms(z[:, q_lora + kv_lora:], gkr_ref[...])
    slab = jnp.concatenate([kr, kr], axis=-1)
    ang = pos_ref[...].astype(F32) * freq_ref[...]
    lane = _lane_iota((1, LANES))
    cos = jnp.cos(ang)
    sin_signed = jnp.where((lane % 64) < 32, -jnp.sin(ang), jnp.sin(ang))
    cos_ref[...] = cos
    sin_ref[...] = sin_signed
    kr_ref[...] = _rope_slab(slab, cos, sin_signed).astype(BF16)


def _mla_q_kernel(cq_ref, w_ref, gn_ref, gr_ref, cos_ref, sin_ref, q_ref, *, heads, scale):
    q = jnp.dot(cq_ref[...], w_ref[...], preferred_element_type=F32)
    lane = _lane_iota((1, LANES))
    low = lane < 64
    cos = cos_ref[...]
    sin_signed = sin_ref[...]
    for h in range(heads):
        n = _rms(q[:, h * LANES:(h + 1) * LANES], gn_ref[...]) * scale
        q_ref[:, h * HEAD_PAD:h * HEAD_PAD + LANES] = n.astype(BF16)
    base = heads * LANES
    for p in range(heads // 2):
        slab = q[:, base + p * LANES:base + (p + 1) * LANES]
        sq = slab * slab
        ss0 = jnp.sum(jnp.where(low, sq, 0.0), axis=-1, keepdims=True)
        ss1 = jnp.sum(jnp.where(low, 0.0, sq), axis=-1, keepdims=True)
        inv = jnp.where(low, lax.rsqrt(ss0 / 64.0 + EPS), lax.rsqrt(ss1 / 64.0 + EPS))
        r = _rope_slab(slab * inv * gr_ref[...], cos, sin_signed) * scale
        zero = jnp.zeros_like(r)
        q_ref[:, (2 * p) * HEAD_PAD + LANES:(2 * p + 1) * HEAD_PAD] = jnp.where(low, r, zero).astype(BF16)
        q_ref[:, (2 * p + 1) * HEAD_PAD + LANES:(2 * p + 2) * HEAD_PAD] = jnp.where(low, zero, r).astype(BF16)


def _mla_kv_kernel(ckv_ref, w_ref, gn_ref, kr_ref, k_ref, v_ref, *, heads):
    kv = jnp.dot(ckv_ref[...], w_ref[...], preferred_element_type=F32)
    lane = _lane_iota((1, LANES))
    low = lane < 64
    kr = kr_ref[...]
    zero = jnp.zeros_like(kr)
    kr_even = jnp.where(low, kr, zero)
    kr_odd = jnp.where(low, zero, kr)
    for h in range(heads):
        n = _rms(kv[:, h * LANES:(h + 1) * LANES], gn_ref[...])
        k_ref[:, h * HEAD_PAD:h * HEAD_PAD + LANES] = n.astype(BF16)
        k_ref[:, h * HEAD_PAD + LANES:(h + 1) * HEAD_PAD] = kr_even if h % 2 == 0 else kr_odd
    v_ref[...] = kv[:, heads * LANES:].astype(BF16)


def _fox_decay_kernel(xn_ref, w_ref, b_ref, c_ref, carry_ref, *, ts):
    @pl.when(pl.program_id(1) == 0)
    def _():
        carry_ref[...] = jnp.zeros_like(carry_ref)

    f = jnp.dot(xn_ref[...], w_ref[...], preferred_element_type=F32) + b_ref[...]
    log_f = jnp.minimum(f, 0.0) - jnp.log1p(jnp.exp(-jnp.abs(f)))
    row = lax.broadcasted_iota(I32, (ts, ts), 0)
    col = lax.broadcasted_iota(I32, (ts, ts), 1)
    tri = jnp.where(col <= row, 1.0, 0.0).astype(BF16)
    a, b, c = _split3(log_f)
    parts = jnp.dot(tri, jnp.concatenate([a, b, c], axis=-1), preferred_element_type=F32)
    local = parts[:, :LANES] + parts[:, LANES:2 * LANES] + parts[:, 2 * LANES:]
    out = carry_ref[...] + local
    c_ref[...] = out
    carry_ref[...] = out[ts - 1:ts, :]


def _fox_proj_kernel(xn_ref, w_ref, g_ref, c_ref, o_ref, *, heads, mode, scale):
    z = jnp.dot(xn_ref[...], w_ref[...], preferred_element_type=F32)
    if mode == "v":
        o_ref[...] = z.astype(BF16)
        return
    if mode == "gate":
        o_ref[...] = z
        return
    lane = _lane_iota((1, LANES))
    cdec = c_ref[...] * LOG2E
    for h in range(heads):
        n = _rms(z[:, h * LANES:(h + 1) * LANES], g_ref[...]) * scale
        o_ref[:, h * HEAD_PAD:h * HEAD_PAD + LANES] = n.astype(BF16)
        ch = cdec[:, h:h + 1]
        if mode == "k":
            ch = -ch
        a, b, c = (t.astype(F32) for t in _split3(ch))
        if mode == "q":
            aug = jnp.where(lane == 0, a, jnp.where(lane == 1, b, jnp.where(lane == 2, c,
                  jnp.where(lane < 6, 1.0, 0.0))))
        else:
            aug = jnp.where(lane < 3, 1.0, jnp.where(lane == 3, a, jnp.where(lane == 4, b,
                  jnp.where(lane == 5, c, 0.0))))
        o_ref[:, h * HEAD_PAD + LANES:(h + 1) * HEAD_PAD] = aug.astype(BF16)


def _attn_tile(q, k_ref, v_ref, j, m_ref, l_ref, acc_ref, *, tk, masked):
    rows = q.shape[0]
    start = pl.multiple_of(j * tk, tk)
    k = k_ref[pl.ds(start, tk), :]
    v = v_ref[pl.ds(start, tk), :]
    s = lax.dot_general(q, k, (((1,), (1,)), ((), ())), preferred_element_type=F32)
    if masked:
        row = lax.broadcasted_iota(I32, (rows, tk), 0)
        col = lax.broadcasted_iota(I32, (rows, tk), 1)
        s = jnp.where(row >= col, s, NEG_BIG)
    m_prev = m_ref[...]
    m_new = jnp.maximum(m_prev, jnp.max(s, axis=-1, keepdims=True))
    alpha = jnp.exp2(m_prev - m_new)
    p = jnp.exp2(s - jnp.concatenate([m_new] * (tk // LANES), axis=1))
    psum = p[:, :LANES]
    for c in range(1, tk // LANES):
        psum = psum + p[:, c * LANES:(c + 1) * LANES]
    l_ref[...] = alpha * l_ref[...] + psum
    acc_ref[...] = alpha * acc_ref[...] + jnp.dot(p.astype(BF16), v, preferred_element_type=F32)
    m_ref[...] = m_new


def _attn_kernel(q_ref, k_ref, v_ref, o_ref, m_a, l_a, acc_a, m_b, l_b, acc_b, *, tk):
    qi = pl.program_id(2)
    m_a[...] = jnp.full_like(m_a, NEG_BIG)
    m_b[...] = jnp.full_like(m_b, NEG_BIG)

    @pl.when((pl.program_id(0) == 0) & (pl.program_id(1) == 0) & (qi == 0))
    def _():
        for ref in (l_a, acc_a, l_b, acc_b):
            ref[...] = jnp.zeros_like(ref)

    q = q_ref[...]
    tile = functools.partial(_attn_tile, k_ref=k_ref, v_ref=v_ref, tk=tk)

    def body(jj, carry):
        tile(q, j=2 * jj, m_ref=m_a, l_ref=l_a, acc_ref=acc_a, masked=False)
        tile(q, j=2 * jj + 1, m_ref=m_b, l_ref=l_b, acc_ref=acc_b, masked=False)
        return carry

    lax.fori_loop(0, qi, body, 0)
    tile(q, j=2 * qi, m_ref=m_a, l_ref=l_a, acc_ref=acc_a, masked=True)
    low = pl.ds(tk, tk)
    tile(q_ref[low, :], j=2 * qi + 1, m_ref=m_b.at[low, :], l_ref=l_b.at[low, :], acc_ref=acc_b.at[low, :],
         masked=True)
    m = jnp.maximum(m_a[...], m_b[...])
    w_a = jnp.exp2(m_a[...] - m)
    w_b = jnp.exp2(m_b[...] - m)
    l = jnp.sum(w_a * l_a[...] + w_b * l_b[...], axis=-1, keepdims=True)
    o_ref[...] = ((w_a * acc_a[...] + w_b * acc_b[...]) / l).astype(BF16)


def _attention(q, k, v, *, batch, seq, heads, tk):
    tq = 2 * tk
    nq = seq // tq
    stat = pltpu.VMEM((tq, LANES), F32)
    return pl.pallas_call(
        functools.partial(_attn_kernel, tk=tk),
        grid=(batch, heads, nq),
        in_specs=[
            pl.BlockSpec((tq, HEAD_PAD), lambda b, h, i: (b * nq + i, h)),
            pl.BlockSpec((seq, HEAD_PAD), lambda b, h, i: (b, h)),
            pl.BlockSpec((seq, LANES), lambda b, h, i: (b, h)),
        ],
        out_specs=pl.BlockSpec((tq, LANES), lambda b, h, i: (b * nq + i, h)),
        out_shape=jax.ShapeDtypeStruct((batch * seq, heads * LANES), BF16),
        scratch_shapes=[stat] * 6,
        compiler_params=_params(("arbitrary", "arbitrary", "arbitrary")),
        name="causal_attention",
    )(q, k, v)


def _oproj_router_kernel(*refs, tm, gated):
    if gated:
        (o_ref, gate_ref, h_ref, wo_ref, gffn_ref, wr_ref, br_ref,
         h1_ref, xn_ref, route_ref, cnt_ref, carry_ref, wsplit_ref) = refs
    else:
        (o_ref, h_ref, wo_ref, gffn_ref, wr_ref, br_ref,
         h1_ref, xn_ref, route_ref, cnt_ref, carry_ref, wsplit_ref) = refs

    @pl.when(pl.program_id(0) == 0)
    def _():
        carry_ref[...] = jnp.zeros_like(carry_ref)
        w = wr_ref[...]
        w_hi = w.astype(BF16)
        wsplit_ref[:, :LANES] = w_hi
        wsplit_ref[:, LANES:] = (w - w_hi.astype(F32)).astype(BF16)

    o = o_ref[...]
    if gated:
        o = (o.astype(F32) * jax.nn.sigmoid(gate_ref[...])).astype(BF16)
    h1 = h_ref[...] + jnp.dot(o, wo_ref[...], preferred_element_type=F32)
    h1_ref[...] = h1
    xn = _rms(h1, gffn_ref[...])
    xn_ref[...] = xn

    x_hi = xn.astype(BF16)
    x_lo = (xn - x_hi.astype(F32)).astype(BF16)
    part = jnp.dot(x_hi, wsplit_ref[...], preferred_element_type=F32)
    logits = (part[:, :LANES] + part[:, LANES:]
              + jnp.dot(x_lo, wsplit_ref[:, :LANES], preferred_element_type=F32)) + br_ref[...]

    lane_i = _lane_iota((tm, LANES))
    lane = lane_i.astype(F32)
    big = 1e6
    is_g = (lane_i >= N_EXPERTS) & (lane_i < N_EXPERTS + N_GROUPS)
    gl = jnp.where(is_g, logits, NEG_BIG)
    gmax = jnp.max(gl, axis=-1, keepdims=True)
    grp = jnp.min(jnp.where(gl == gmax, lane, big), axis=-1, keepdims=True) - N_EXPERTS
    p_g = 1.0 / jnp.sum(jnp.where(is_g, jnp.exp(gl - gmax), 0.0), axis=-1, keepdims=True)

    in_grp = (lane_i // EXPERTS_PER_GROUP).astype(F32) == grp
    el = jnp.where(in_grp, logits, NEG_BIG)
    emax = jnp.max(el, axis=-1, keepdims=True)
    pe = jnp.where(in_grp, jnp.exp(el - emax), 0.0)
    pe = pe / jnp.sum(pe, axis=-1, keepdims=True)
    pm = jnp.where(in_grp, pe, -1.0)
    p1 = jnp.max(pm, axis=-1, keepdims=True)
    i1 = jnp.min(jnp.where(pm == p1, lane, big), axis=-1, keepdims=True)
    pm2 = jnp.where(lane == i1, -1.0, pm)
    p2 = jnp.max(pm2, axis=-1, keepdims=True)
    i2 = jnp.min(jnp.where(pm2 == p2, lane, big), axis=-1, keepdims=True)
    den = p1 + p2
    g1 = p_g * p1 / den
    g2 = p_g * p2 / den

    hit1 = lane == i1
    hit2 = lane == i2
    oh1 = jnp.where(hit1, 1.0, 0.0)
    oh2 = jnp.where(hit2, 1.0, 0.0)
    row = lax.broadcasted_iota(I32, (tm, tm), 0)
    col = lax.broadcasted_iota(I32, (tm, tm), 1)
    tri = jnp.where(col < row, 1.0, 0.0).astype(BF16)
    cum1 = jnp.dot(tri, oh1.astype(BF16), preferred_element_type=F32)
    cum2 = jnp.dot(tri, oh2.astype(BF16), preferred_element_type=F32)
    tot1 = jnp.sum(oh1, axis=0, keepdims=True)
    tot2 = jnp.sum(oh2, axis=0, keepdims=True)
    carry = carry_ref[...]
    r1 = jnp.sum(jnp.where(hit1, carry + cum1, 0.0), axis=-1, keepdims=True)
    r2 = jnp.sum(jnp.where(hit2, carry + tot1 + cum2, 0.0), axis=-1, keepdims=True)
    new_carry = carry + tot1 + tot2
    carry_ref[...] = new_carry
    cnt_ref[...] = new_carry

    zero = jnp.zeros((tm, LANES), F32)
    route = jnp.where(lane_i == 0, i1, jnp.where(lane_i == 1, i2,
            jnp.where(lane_i == 2, r1, jnp.where(lane_i == 3, r2,
            jnp.where(lane_i == 4, g1, jnp.where(lane_i == 5, g2, zero))))))
    route_ref[...] = route


def _oproj_router(o, gate, h, wo, gffn, wr, br, *, tm):
    tokens, d = h.shape
    gated = gate is not None
    row_spec = pl.BlockSpec((tm, d), lambda i: (i, 0))
    const = lambda shape: pl.BlockSpec(shape, lambda i: (0, 0))
    in_specs = [row_spec] + ([row_spec] if gated else []) + [
        row_spec, const(wo.shape), const((1, d)), const(wr.shape), const((1, LANES))]
    args = [o] + ([gate] if gated else []) + [h, wo, gffn, wr, br]
    return pl.pallas_call(
        functools.partial(_oproj_router_kernel, tm=tm, gated=gated),
        grid=(tokens // tm,),
        in_specs=in_specs,
        out_specs=[row_spec, row_spec, pl.BlockSpec((tm, LANES), lambda i: (i, 0)), const((1, LANES))],
        out_shape=[jax.ShapeDtypeStruct((tokens, d), F32), jax.ShapeDtypeStruct((tokens, d), F32),
                   jax.ShapeDtypeStruct((tokens, LANES), F32), jax.ShapeDtypeStruct((1, LANES), F32)],
        scratch_shapes=[pltpu.VMEM((1, LANES), F32), pltpu.VMEM((d, 2 * LANES), BF16)],
        compiler_params=_params(("arbitrary",)),
        name="oproj_router",
    )(*args)


def _plan_kernel(cnt_ref, plan_ref, blk_ref, *, n_blocks):
    cnt = cnt_ref[...]
    nblk = jnp.floor((cnt + (MOE_ROWS - 1)) * (1.0 / MOE_ROWS))
    row = lax.broadcasted_iota(I32, (LANES, LANES), 0)
    col = lax.broadcasted_iota(I32, (LANES, LANES), 1)
    upper = jnp.where(row < col, 1.0, 0.0).astype(BF16)
    nb8 = jnp.broadcast_to(nblk, (8, LANES)).astype(BF16)
    bstart = jnp.dot(nb8, upper, preferred_element_type=F32)[0:1, :]
    bend = bstart + nblk
    lane = _lane_iota((1, LANES))
    total = jnp.sum(nblk, axis=-1, keepdims=True)
    sub = lax.broadcasted_iota(I32, (8, LANES), 0)
    plan = jnp.where(sub == 0, jnp.broadcast_to(bstart, (8, LANES)),
           jnp.where(sub == 1, jnp.broadcast_to(nblk, (8, LANES)),
           jnp.where(sub == 2, jnp.broadcast_to(total, (8, LANES)), 0.0)))
    plan_ref[...] = plan
    bidx = lax.broadcasted_iota(I32, (n_blocks, LANES), 0).astype(F32)
    done = jnp.where((_lane_iota((n_blocks, LANES)) < N_EXPERTS) & (bend <= bidx), 1.0, 0.0)
    last_used = jnp.max(jnp.where(nblk > 0.0, lane.astype(F32), 0.0), axis=-1, keepdims=True)
    e_of_b = jnp.minimum(jnp.sum(done, axis=-1, keepdims=True), last_used)
    blk_ref[...] = jnp.broadcast_to(e_of_b, (n_blocks, LANES))


def _dest_kernel(route_ref, plan_ref, dest_ref, *, tm):
    route = route_ref[...]
    bstart = plan_ref[0:1, :] * float(MOE_ROWS)
    lane = _lane_iota((tm, LANES)).astype(F32)
    e1 = route[:, 0:1]
    e2 = route[:, 1:2]
    d1 =jnp.sum(jnp.where(lane == e1, bstart, 0.0), axis=-1, keepdims=True) + route[:, 2:3]
    d2 = jnp.sum(jnp.where(lane == e2, bstart, 0.0), axis=-1, keepdims=True) + route[:, 3:4]
    dest_ref[...] = jnp.where(lane == 0, d1, jnp.where(lane == 1, d2, 0.0))


def _invert_kernel(dest_ref, src_ref, *, tokens, n_rows):
    def zero(i, c):
        src_ref[i] = 0
        return c

    lax.fori_loop(0, n_rows, zero, 0, unroll=8)

    def put(t, c):
        src_ref[dest_ref[2 * t]] = t
        src_ref[dest_ref[2 * t + 1]] = t
        return c

    lax.fori_loop(0, tokens, put, 0, unroll=8)


def _expert_kernel(blk_ref, nused_ref, bstart_ref, nblk_ref, src_ref, x_hbm, wgu_hbm, wdn_hbm, y_ref,
                   xbuf, wgu_f32, wdn_f32, wgu_bf, wdn_bf, slot_ref, sems, sems_x, *, d_expert):
    b = pl.program_id(0)

    def gather(blk, slot):
        base = blk * MOE_ROWS

        def start_row(r, c):
            pltpu.make_async_copy(x_hbm.at[pl.ds(src_ref[base + r], 1), :], xbuf.at[slot, pl.ds(r, 1), :],
                                  sems_x.at[slot]).start()
            return c

        lax.fori_loop(0, MOE_ROWS, start_row, 0, unroll=8)

    def weight_copies(e, slot):
        return (pltpu.make_async_copy(wgu_hbm.at[e], wgu_f32.at[slot], sems.at[0, slot]),
                pltpu.make_async_copy(wdn_hbm.at[e], wdn_f32.at[slot], sems.at[1, slot]))

    @pl.when(b == 0)
    def _():
        slot_ref[0] = 0
        for c in weight_copies(blk_ref[0], 0):
            c.start()
        gather(0, 0)

    @pl.when(b < nused_ref[0])
    def _():
        xslot = b % 2

        @pl.when(b + 1 < nused_ref[0])
        def _():
            gather(b + 1, 1 - xslot)

        e = blk_ref[b]
        first = (b == 0) | (e != blk_ref[jnp.maximum(b - 1, 0)])

        @pl.when(first)
        def _():
            slot = slot_ref[0]
            nxt = bstart_ref[e] + nblk_ref[e]

            @pl.when(nxt < nused_ref[0])
            def _():
                for c in weight_copies(blk_ref[nxt], 1 - slot):
                    c.start()

            for c in weight_copies(e, slot):
                c.wait()
            wgu_bf[...] = wgu_f32[slot].astype(BF16)
            wdn_bf[...] = wdn_f32[slot].astype(BF16)
            slot_ref[0] = 1 - slot

        pltpu.make_async_copy(x_hbm.at[pl.ds(0, MOE_ROWS), :], xbuf.at[xslot], sems_x.at[xslot]).wait()
        x = xbuf[xslot].astype(BF16)
        gu = jnp.dot(x, wgu_bf[...], preferred_element_type=F32)
        g = gu[:, :d_expert]
        act = (g * jax.nn.sigmoid(g)) * gu[:, d_expert:]
        y_ref[...] = jnp.dot(act.astype(BF16), wdn_bf[...], preferred_element_type=F32)

    @pl.when(b >= nused_ref[0])
    def _():
        y_ref[...] = jnp.zeros_like(y_ref)


def _combine_kernel(dest_ref, dest_next_ref, y_hbm, route_ref, h_ref, g_ref, out_ref, xn_ref, rows_ref, sems,
                    *, tc, with_norm):
    step = pl.program_id(0)
    nsteps = pl.num_programs(0)
    slot = step % 2

    def issue(idx_ref, to_slot):
        def start_token(t, c):
            for k in range(2):
                pltpu.make_async_copy(y_hbm.at[pl.ds(idx_ref[2 * t + k], 1), :],
                                      rows_ref.at[to_slot, k, pl.ds(t, 1), :], sems.at[to_slot]).start()
            return c
        lax.fori_loop(0, tc, start_token, 0, unroll=4)

    @pl.when(step == 0)
    def _():
        issue(dest_ref, 0)

    @pl.when(step + 1 < nsteps)
    def _():
        issue(dest_next_ref, 1 - slot)

    for k in range(2):
        pltpu.make_async_copy(y_hbm.at[pl.ds(0, tc), :], rows_ref.at[slot, k], sems.at[slot]).wait()
    route = route_ref[...]
    out = h_ref[...] + (rows_ref[slot, 0] * route[:, 4:5] + rows_ref[slot, 1] * route[:, 5:6])
    out_ref[...] = out
    if with_norm:
        xn_ref[...] = _rms(out, g_ref[...]).astype(BF16)
    else:
        xn_ref[...] = jnp.zeros_like(xn_ref)


def _moe(xn, route, cnt, h1, w_gate_up, w_down, next_gain):
    tokens, d = xn.shape
    d_expert = w_down.shape[1]
    n_blocks = (2 * tokens + N_EXPERTS * (MOE_ROWS - 1) + MOE_ROWS - 1) // MOE_ROWS
    assert n_blocks <= 256, "block counts must stay exact in bf16"
    nb_pad = (n_blocks + 7) // 8 * 8

    plan, blk = pl.pallas_call(
        functools.partial(_plan_kernel, n_blocks=nb_pad),
        out_shape=[jax.ShapeDtypeStruct((8, LANES), F32), jax.ShapeDtypeStruct((nb_pad, LANES), F32)],
        name="moe_plan",
    )(cnt)

    tm = 512
    dest = pl.pallas_call(
        functools.partial(_dest_kernel, tm=tm),
        grid=(tokens // tm,),
        in_specs=[pl.BlockSpec((tm, LANES), lambda i: (i, 0)), pl.BlockSpec((8, LANES), lambda i: (0, 0))],
        out_specs=pl.BlockSpec((tm, LANES), lambda i: (i, 0)),
        out_shape=jax.ShapeDtypeStruct((tokens, LANES), F32),
        compiler_params=_params(("parallel",)),
        name="moe_dest",
    )(route, plan)

    dest_flat = dest[:, :2].astype(I32).reshape(-1)
    bstart = plan[0, :N_EXPERTS].astype(I32)
    nblk = plan[1, :N_EXPERTS].astype(I32)
    n_used = plan[2, :1].astype(I32)
    blk_e = blk[:n_blocks, 0].astype(I32)

    smem = pltpu.SMEM
    n_rows = n_blocks * MOE_ROWS
    src = pl.pallas_call(
        functools.partial(_invert_kernel, tokens=tokens, n_rows=n_rows),
        in_specs=[pl.BlockSpec(memory_space=smem)],
        out_specs=pl.BlockSpec(memory_space=smem),
        out_shape=jax.ShapeDtypeStruct((n_rows,), I32),
        name="moe_invert",
    )(dest_flat)

    y = pl.pallas_call(
        functools.partial(_expert_kernel, d_expert=d_expert),
        grid_spec=pltpu.PrefetchScalarGridSpec(
            num_scalar_prefetch=5,
            grid=(n_blocks,),
            in_specs=[
                pl.BlockSpec(memory_space=pl.ANY),
                pl.BlockSpec(memory_space=pl.ANY),
                pl.BlockSpec(memory_space=pl.ANY),
            ],
            out_specs=pl.BlockSpec((MOE_ROWS, d), lambda b, *_: (b, 0)),
            scratch_shapes=[pltpu.VMEM((2, MOE_ROWS, d), F32),
                            pltpu.VMEM((2, d, 2 * d_expert), F32), pltpu.VMEM((2, d_expert, d), F32),
                            pltpu.VMEM((d, 2 * d_expert), BF16), pltpu.VMEM((d_expert, d), BF16),
                            pltpu.SMEM((1,), I32), pltpu.SemaphoreType.DMA((2, 2)), pltpu.SemaphoreType.DMA((2,))],
        ),
        out_shape=jax.ShapeDtypeStruct((n_blocks * MOE_ROWS, d), F32),
        compiler_params=_params(("arbitrary",)),
        name="moe_experts",
    )(blk_e, n_used, bstart, nblk, src, xn, w_gate_up, w_down)

    tc = 256
    with_norm = next_gain is not None
    gain = next_gain if with_norm else jnp.ones((1, d), F32)
    row_spec = pl.BlockSpec((tc, d), lambda i: (i, 0))
    out, xn_next = pl.pallas_call(
        functools.partial(_combine_kernel, tc=tc, with_norm=with_norm),
        grid=(tokens // tc,),
        in_specs=[
            pl.BlockSpec((2 * tc,), lambda i: (i,), memory_space=smem),
            pl.BlockSpec((2 * tc,), lambda i: (jnp.minimum(i + 1, tokens // tc - 1),), memory_space=smem),
            pl.BlockSpec(memory_space=pl.ANY),
            pl.BlockSpec((tc, LANES), lambda i: (i, 0)),
            row_spec,
            pl.BlockSpec((1, d), lambda i: (0, 0)),
        ],
        out_specs=[row_spec, row_spec if with_norm else pl.BlockSpec((8, LANES), lambda i: (0, 0))],
        out_shape=[jax.ShapeDtypeStruct((tokens, d), F32),
                   jax.ShapeDtypeStruct((tokens, d), BF16) if with_norm else jax.ShapeDtypeStruct((8, LANES), BF16)],
        scratch_shapes=[pltpu.VMEM((2, 2, tc, d), F32), pltpu.SemaphoreType.DMA((2,))],
        compiler_params=_params(("arbitrary",)),
        name="moe_combine",
    )(dest_flat, dest_flat, y, route, h1, gain)
    return out, (xn_next if with_norm else None)


def _router_weights(w_group, b_group, w_expert, b_expert):
    d = w_group.shape[0]
    pad = LANES - N_EXPERTS - N_GROUPS
    w = jnp.concatenate([w_expert, w_group, jnp.zeros((d, pad), F32)], axis=1)
    b = jnp.concatenate([b_expert, b_group, jnp.zeros((pad,), F32)])[None, :]
    return w, b


def _row(v):
    return v.astype(F32)[None, :]


def kernel(x, positions, l0_norm_mix, l0_mla_w_in, l0_mla_q_lat_norm, l0_mla_kv_lat_norm, l0_mla_w_uq, l0_mla_w_ukv, l0_mla_qk_gain, l0_mla_w_o, l0_norm_ffn, l0_router_group, l0_router_group_bias, l0_router_expert, l0_router_expert_bias, l0_w_gate_up, l0_w_down, l1_norm_mix, l1_fox_w_in, l1_fox_forget_bias, l1_fox_qk_gain, l1_fox_w_o, l1_norm_ffn, l1_router_group, l1_router_group_bias, l1_router_expert, l1_router_expert_bias, l1_w_gate_up, l1_w_down):
    batch, seq, d = x.shape
    tokens = batch * seq
    q_lora = l0_mla_q_lat_norm.shape[0]
    kv_lora = l0_mla_kv_lat_norm.shape[0]
    rope = l0_mla_w_in.shape[1] - q_lora - kv_lora
    nope = l0_mla_qk_gain.shape[1] - rope
    heads = l0_mla_w_uq.shape[1] // (nope + rope)
    assert rope == 64 and nope == LANES and heads % 2 == 0
    tk_attn = min(512, seq // 2)

    h0 = x.reshape(tokens, d)
    pos = positions.reshape(tokens, 1)

    half = rope // 2
    freqs = ROPE_THETA ** (-jnp.arange(half, dtype=F32) / half)
    freq_row = jnp.tile(freqs, LANES // half)[None, :]
    gkr = _row(l0_mla_qk_gain[1, nope:])
    tm = 512
    row = lambda w: pl.BlockSpec((tm, w), lambda i: (i, 0))
    const = lambda shape: pl.BlockSpec(shape, lambda i: (0, 0))
    win = l0_mla_w_in.astype(BF16)
    cq, ckv, kr2, cos_t, sin_t = pl.pallas_call(
        functools.partial(_mla_in_kernel, q_lora=q_lora, kv_lora=kv_lora),
        grid=(tokens // tm,),
        in_specs=[row(d), row(1), const((1, d)), const(win.shape), const((1, q_lora)), const((1, kv_lora)),
                  const((1, rope)), const((1, LANES))],
        out_specs=[row(q_lora), row(kv_lora), row(LANES), row(LANES), row(LANES)],
        out_shape=[jax.ShapeDtypeStruct((tokens, q_lora), BF16), jax.ShapeDtypeStruct((tokens, kv_lora), BF16),
                   jax.ShapeDtypeStruct((tokens, LANES), BF16), jax.ShapeDtypeStruct((tokens, LANES), F32),
                   jax.ShapeDtypeStruct((tokens, LANES), F32)],
        compiler_params=_params(("parallel",)),
        name="mla_in",
    )(h0, pos, _row(l0_norm_mix), win, _row(l0_mla_q_lat_norm), _row(l0_mla_kv_lat_norm), gkr, freq_row)

    wuq = l0_mla_w_uq.reshape(q_lora, heads, nope + rope)
    wuq = jnp.concatenate([wuq[:, :, :nope].reshape(q_lora, heads * nope),
                           wuq[:, :, nope:].reshape(q_lora, heads * rope)], axis=1).astype(BF16)
    scale = LOG2E / math.sqrt(nope + rope)
    q_aug = pl.pallas_call(
        functools.partial(_mla_q_kernel, heads=heads, scale=scale),
        grid=(tokens // tm,),
        in_specs=[row(q_lora), const(wuq.shape), const((1, nope)), const((1, LANES)), row(LANES), row(LANES)],
        out_specs=row(heads * HEAD_PAD),
        out_shape=jax.ShapeDtypeStruct((tokens, heads * HEAD_PAD), BF16),
        compiler_params=_params(("parallel",)),
        name="mla_q",
    )(cq, wuq, _row(l0_mla_qk_gain[0, :nope]), _row(jnp.tile(l0_mla_qk_gain[0, nope:], 2)), cos_t, sin_t)

    v_dim = l0_mla_w_ukv.shape[1] // heads - nope
    assert v_dim == LANES
    wukv = l0_mla_w_ukv.reshape(kv_lora, heads, nope + v_dim)
    wukv = jnp.concatenate([wukv[:, :, :nope].reshape(kv_lora, heads * nope),
                            wukv[:, :, nope:].reshape(kv_lora, heads * v_dim)], axis=1).astype(BF16)
    k_aug, v0 = pl.pallas_call(
        functools.partial(_mla_kv_kernel, heads=heads),
        grid=(tokens // tm,),
        in_specs=[row(kv_lora), const(wukv.shape), const((1, nope)), row(LANES)],
        out_specs=[row(heads * HEAD_PAD), row(heads * v_dim)],
        out_shape=[jax.ShapeDtypeStruct((tokens, heads * HEAD_PAD), BF16),
                   jax.ShapeDtypeStruct((tokens, heads * v_dim), BF16)],
        compiler_params=_params(("parallel",)),
        name="mla_kv",
    )(ckv, wukv, _row(l0_mla_qk_gain[1, :nope]), kr2)

    o0 = _attention(q_aug, k_aug, v0, batch=batch, seq=seq, heads=heads, tk=tk_attn)

    wr0, br0 = _router_weights(l0_router_group, l0_router_group_bias, l0_router_expert, l0_router_expert_bias)
    h1, xn1, route0, cnt0 = _oproj_router(o0, None, h0, l0_mla_w_o.astype(BF16), _row(l0_norm_ffn), wr0, br0, tm=512)
    h2, xn2 = _moe(xn1, route0, cnt0, h1, l0_w_gate_up, l0_w_down, _row(l1_norm_mix))

    fheads = l1_fox_forget_bias.shape[0]
    hd = l1_fox_qk_gain.shape[1]
    assert hd == LANES
    hh = fheads * hd
    w1 = l1_fox_w_in
    wq1, wk1, wv1 = (w1[:, i * hh:(i + 1) * hh].astype(BF16) for i in range(3))
    wf1 = jnp.concatenate([w1[:, 3 * hh:3 * hh + fheads], jnp.zeros((d, LANES - fheads), F32)], axis=1).astype(BF16)
    wg1 = w1[:, 3 * hh + fheads:].astype(BF16)
    fbias = jnp.concatenate([l1_fox_forget_bias.astype(F32), jnp.zeros((LANES - fheads,), F32)])[None, :]

    ts = min(512, seq)
    ns = seq // ts
    cdec = pl.pallas_call(
        functools.partial(_fox_decay_kernel, ts=ts),
        grid=(batch, ns),
        in_specs=[pl.BlockSpec((ts, d), lambda b, i: (b * ns + i, 0)), pl.BlockSpec((d, LANES), lambda b, i: (0, 0)),
                  pl.BlockSpec((1, LANES), lambda b, i: (0, 0))],
        out_specs=pl.BlockSpec((ts, LANES), lambda b, i: (b * ns + i, 0)),
        out_shape=jax.ShapeDtypeStruct((tokens, LANES), F32),
        scratch_shapes=[pltpu.VMEM((1, LANES), F32)],
        compiler_params=_params(("arbitrary", "arbitrary")),
        name="fox_decay",
    )(xn2, wf1, fbias)

    def fox_proj(w, gain, mode, out_width, out_dtype):
        return pl.pallas_call(
            functools.partial(_fox_proj_kernel, heads=fheads, mode=mode,
                              scale=(LOG2E / math.sqrt(hd)) if mode == "q" else 1.0),
            grid=(tokens // tm,),
            in_specs=[row(d), const(w.shape), const((1, hd)), row(LANES)],
            out_specs=row(out_width),
            out_shape=jax.ShapeDtypeStruct((tokens, out_width), out_dtype),
            compiler_params=_params(("parallel",)),
            name="fox_proj_" + mode,
        )(xn2, w, gain, cdec)

    q1 = fox_proj(wq1, _row(l1_fox_qk_gain[0]), "q", fheads * HEAD_PAD, BF16)
    k1 = fox_proj(wk1, _row(l1_fox_qk_gain[1]), "k", fheads * HEAD_PAD, BF16)
    v1 = fox_proj(wv1, _row(l1_fox_qk_gain[0]), "v", hh, BF16)
    gate1 = fox_proj(wg1, _row(l1_fox_qk_gain[0]), "gate", hh, F32)

    o1 = _attention(q1, k1, v1, batch=batch, seq=seq, heads=fheads, tk=tk_attn)

    wr1, br1 = _router_weights(l1_router_group, l1_router_group_bias, l1_router_expert, l1_router_expert_bias)
    h3, xn3, route1, cnt1 = _oproj_router(o1, gate1, h2, l1_fox_w_o.astype(BF16), _row(l1_norm_ffn), wr1, br1, tm=512)
    h4, _ = _moe(xn3, route1, cnt1, h3, l1_w_gate_up, l1_w_down, None)
    return h4.reshape(batch, seq, d)
```

```python
import functools
import math

import jax
import jax.numpy as jnp
from jax import lax
from jax.experimental import pallas as pl
from jax.experimental.pallas import tpu as pltpu

F32 = jnp.float32
BF16 = jnp.bfloat16
I32 = jnp.int32

EPS = 1e-6
ROPE_THETA = 10000.0
LANES = 128
HEAD_PAD = 256
N_GROUPS = 8
EXPERTS_PER_GROUP = 8
N_EXPERTS = N_GROUPS * EXPERTS_PER_GROUP
MOE_ROWS = 256
NEG_BIG = -1e30
LOG2E = math.log2(math.e)
VMEM_LIMIT = 52 * 1024 * 1024


def _params(sem, vmem=VMEM_LIMIT):
    return pltpu.CompilerParams(dimension_semantics=sem, vmem_limit_bytes=vmem)


def _rms(x, gain):
    ms = jnp.mean(x * x, axis=-1, keepdims=True)
    return x * lax.rsqrt(ms + EPS) * gain


def _lane_iota(shape):
    return lax.broadcasted_iota(I32, shape, len(shape) - 1)


def _rope_slab(x, cos, sin_signed):
    lane = _lane_iota((1, LANES))
    low = (lane % 64) < 32
    swapped = jnp.where(low, pltpu.roll(x, 96, 1), pltpu.roll(x, 32, 1))
    return x * cos + swapped * sin_signed


def _split3(x):
    a = x.astype(BF16)
    r = x - a.astype(F32)
    b = r.astype(BF16)
    c = (r - b.astype(F32)).astype(BF16)
    return a, b, c


def _mla_in_kernel(h_ref, pos_ref, gmix_ref, win_ref, gq_ref, gkv_ref, gkr_ref, freq_ref,
                   cq_ref, ckv_ref, kr_ref, cos_ref, sin_ref, *, q_lora, kv_lora):
    xn = _rms(h_ref[...], gmix_ref[...]).astype(BF16)
    z = jnp.dot(xn, win_ref[...], preferred_element_type=F32)
    cq_ref[...] = _rms(z[:, :q_lora], gq_ref[...]).astype(BF16)
    ckv_ref[...] = _rms(z[:, q_lora:q_lora + kv_lora], gkv_ref[...]).astype(BF16)
    kr = _rms(z[:, q_lora + kv_lora:], gkr_ref[...])
    slab = jnp.concatenate([kr, kr], axis=-1)
    ang = pos_ref[...].astype(F32) * freq_ref[...]
    lane = _lane_iota((1, LANES))
    cos = jnp.cos(ang)
    sin_signed = jnp.where((lane % 64) < 32, -jnp.sin(ang), jnp.sin(ang))
    cos_ref[...] = cos
    sin_ref[...] = sin_signed
    kr_ref[...] = _rope_slab(slab, cos, sin_signed).astype(BF16)


def _mla_q_kernel(cq_ref, w_ref, gn_ref, gr_ref, cos_ref, sin_ref, q_ref, *, heads, scale):
    q = jnp.dot(cq_ref[...], w_ref[...], preferred_element_type=F32)
    lane = _lane_iota((1, LANES))
    low = lane < 64
    cos = cos_ref[...]
    sin_signed = sin_ref[...]
    for h in range(heads):
        n = _rms(q[:, h * LANES:(h + 1) * LANES], gn_ref[...]) * scale
        q_ref[:, h * HEAD_PAD:h * HEAD_PAD + LANES] = n.astype(BF16)
    base = heads * LANES
    for p in range(heads // 2):
        slab = q[:, base + p * LANES:base + (p + 1) * LANES]
        sq = slab * slab
        ss0 = jnp.sum(jnp.where(low, sq, 0.0), axis=-1, keepdims=True)
        ss1 = jnp.sum(jnp.where(low, 0.0, sq), axis=-1, keepdims=True)
        inv = jnp.where(low, lax.rsqrt(ss0 / 64.0 + EPS), lax.rsqrt(ss1 / 64.0 + EPS))
        r = _rope_slab(slab * inv * gr_ref[...], cos, sin_signed) * scale
        zero = jnp.zeros_like(r)
        q_ref[:, (2 * p) * HEAD_PAD + LANES:(2 * p + 1) * HEAD_PAD] = jnp.where(low, r, zero).astype(BF16)
        q_ref[:, (2 * p + 1) * HEAD_PAD + LANES:(2 * p + 2) * HEAD_PAD] = jnp.where(low, zero, r).astype(BF16)


def _mla_kv_kernel(ckv_ref, w_ref, gn_ref, kr_ref, k_ref, v_ref, *, heads):
    kv = jnp.dot(ckv_ref[...], w_ref[...], preferred_element_type=F32)
    lane = _lane_iota((1, LANES))
    low = lane < 64
    kr = kr_ref[...]
    zero = jnp.zeros_like(kr)
    kr_even = jnp.where(low, kr, zero)
    kr_odd = jnp.where(low, zero, kr)
    for h in range(heads):
        n = _rms(kv[:, h * LANES:(h + 1) * LANES], gn_ref[...])
        k_ref[:, h * HEAD_PAD:h * HEAD_PAD + LANES] = n.astype(BF16)
        k_ref[:, h * HEAD_PAD + LANES:(h + 1) * HEAD_PAD] = kr_even if h % 2 == 0 else kr_odd
    v_ref[...] = kv[:, heads * LANES:].astype(BF16)


def _fox_decay_kernel(xn_ref, w_ref, b_ref, c_ref, carry_ref, *, ts):
    @pl.when(pl.program_id(1) == 0)
    def _():
        carry_ref[...] = jnp.zeros_like(carry_ref)

    f = jnp.dot(xn_ref[...], w_ref[...], preferred_element_type=F32) + b_ref[...]
    log_f = jnp.minimum(f, 0.0) - jnp.log1p(jnp.exp(-jnp.abs(f)))
    row = lax.broadcasted_iota(I32, (ts, ts), 0)
    col = lax.broadcasted_iota(I32, (ts, ts), 1)
    tri = jnp.where(col <= row, 1.0, 0.0).astype(BF16)
    a, b, c = _split3(log_f)
    parts = jnp.dot(tri, jnp.concatenate([a, b, c], axis=-1), preferred_element_type=F32)
    local = parts[:, :LANES] + parts[:, LANES:2 * LANES] + parts[:, 2 * LANES:]
    out = carry_ref[...] + local
    c_ref[...] = out
    carry_ref[...] = out[ts - 1:ts, :]


def _fox_proj_kernel(xn_ref, w_ref, g_ref, c_ref, o_ref, *, heads, mode, scale):
    z = jnp.dot(xn_ref[...], w_ref[...], preferred_element_type=F32)
    if mode == "v":
        o_ref[...] = z.astype(BF16)
        return
    if mode == "gate":
        o_ref[...] = z
        return
    lane = _lane_iota((1, LANES))
    cdec = c_ref[...] * LOG2E
    for h in range(heads):
        n = _rms(z[:, h * LANES:(h + 1) * LANES], g_ref[...]) * scale
        o_ref[:, h * HEAD_PAD:h * HEAD_PAD + LANES] = n.astype(BF16)
        ch = cdec[:, h:h + 1]
        if mode == "k":
            ch = -ch
        a, b, c = (t.astype(F32) for t in _split3(ch))
        if mode == "q":
            aug = jnp.where(lane == 0, a, jnp.where(lane == 1, b, jnp.where(lane == 2, c,
                  jnp.where(lane < 6, 1.0, 0.0))))
        else:
            aug = jnp.where(lane < 3, 1.0, jnp.where(lane == 3, a, jnp.where(lane == 4, b,
                  jnp.where(lane == 5, c, 0.0))))
        o_ref[:, h * HEAD_PAD + LANES:(h + 1) * HEAD_PAD] = aug.astype(BF16)


def _attn_tile(q, k_ref, v_ref, j, m_ref, l_ref, acc_ref, *, tk, masked):
    rows = q.shape[0]
    start = pl.multiple_of(j * tk, tk)
    k = k_ref[pl.ds(start, tk), :]
    v = v_ref[pl.ds(start, tk), :]
    s = lax.dot_general(q, k, (((1,), (1,)), ((), ())), preferred_element_type=F32)
    if masked:
        row = lax.broadcasted_iota(I32, (rows, tk), 0)
        col = lax.broadcasted_iota(I32, (rows, tk), 1)
        s = jnp.where(row >= col, s, NEG_BIG)
    m_prev = m_ref[...]
    m_new = jnp.maximum(m_prev, jnp.max(s, axis=-1, keepdims=True))
    alpha = jnp.exp2(m_prev - m_new)
    p = jnp.exp2(s - jnp.concatenate([m_new] * (tk // LANES), axis=1))
    psum = p[:, :LANES]
    for c in range(1, tk // LANES):
        psum = psum + p[:, c * LANES:(c + 1) * LANES]
    l_ref[...] = alpha * l_ref[...] + psum
    acc_ref[...] = alpha * acc_ref[...] + jnp.dot(p.astype(BF16), v, preferred_element_type=F32)
    m_ref[...] = m_new


def _attn_kernel(q_ref, k_ref, v_ref, o_ref, m_a, l_a, acc_a, m_b, l_b, acc_b, *, tk):
    qi = pl.program_id(2)
    m_a[...] = jnp.full_like(m_a, NEG_BIG)
    m_b[...] = jnp.full_like(m_b, NEG_BIG)

    @pl.when((pl.program_id(0) == 0) & (pl.program_id(1) == 0) & (qi == 0))
    def _():
        for ref in (l_a, acc_a, l_b, acc_b):
            ref[...] = jnp.zeros_like(ref)

    q = q_ref[...]
    tile = functools.partial(_attn_tile, k_ref=k_ref, v_ref=v_ref, tk=tk)

    def body(jj, carry):
        tile(q, j=2 * jj, m_ref=m_a, l_ref=l_a, acc_ref=acc_a, masked=False)
        tile(q, j=2 * jj + 1, m_ref=m_b, l_ref=l_b, acc_ref=acc_b, masked=False)
        return carry

    lax.fori_loop(0, qi, body, 0)
    tile(q, j=2 * qi, m_ref=m_a, l_ref=l_a, acc_ref=acc_a, masked=True)
    low = pl.ds(tk, tk)
    tile(q_ref[low, :], j=2 * qi + 1, m_ref=m_b.at[low, :], l_ref=l_b.at[low, :], acc_ref=acc_b.at[low, :],
         masked=True)
    m = jnp.maximum(m_a[...], m_b[...])
    w_a = jnp.exp2(m_a[...] - m)
    w_b = jnp.exp2(m_b[...] - m)
    l = jnp.sum(w_a * l_a[...] + w_b * l_b[...], axis=-1, keepdims=True)
    o_ref[...] = ((w_a * acc_a[...] + w_b * acc_b[...]) / l).astype(BF16)


def _attention(q, k, v, *, batch, seq, heads, tk):
    tq = 2 * tk
    nq = seq // tq
    stat = pltpu.VMEM((tq, LANES), F32)
    return pl.pallas_call(
        functools.partial(_attn_kernel, tk=tk),
        grid=(batch, heads, nq),
        in_specs=[
            pl.BlockSpec((tq, HEAD_PAD), lambda b, h, i: (b * nq + i, h)),
            pl.BlockSpec((seq, HEAD_PAD), lambda b, h, i: (b, h)),
            pl.BlockSpec((seq, LANES), lambda b, h, i: (b, h)),
        ],
        out_specs=pl.BlockSpec((tq, LANES), lambda b, h, i: (b * nq + i, h)),
        out_shape=jax.ShapeDtypeStruct((batch * seq, heads * LANES), BF16),
        scratch_shapes=[stat] * 6,
        compiler_params=_params(("arbitrary", "arbitrary", "arbitrary")),
        name="causal_attention",
    )(q, k, v)


def _oproj_router_kernel(*refs, tm, gated):
    if gated:
        (o_ref, gate_ref, h_ref, wo_ref, gffn_ref, wr_ref, br_ref,
         h1_ref, xn_ref, route_ref, cnt_ref, carry_ref, wsplit_ref) = refs
    else:
        (o_ref, h_ref, wo_ref, gffn_ref, wr_ref, br_ref,
         h1_ref, xn_ref, route_ref, cnt_ref, carry_ref, wsplit_ref) = refs

    @pl.when(pl.program_id(0) == 0)
    def _():
        carry_ref[...] = jnp.zeros_like(carry_ref)
        w = wr_ref[...]
        w_hi = w.astype(BF16)
        wsplit_ref[:, :LANES] = w_hi
        wsplit_ref[:, LANES:] = (w - w_hi.astype(F32)).astype(BF16)

    o = o_ref[...]
    if gated:
        o = (o.astype(F32) * jax.nn.sigmoid(gate_ref[...])).astype(BF16)
    h1 = h_ref[...] + jnp.dot(o, wo_ref[...], preferred_element_type=F32)
    h1_ref[...] = h1
    xn = _rms(h1, gffn_ref[...])
    xn_ref[...] = xn

    x_hi = xn.astype(BF16)
    x_lo = (xn - x_hi.astype(F32)).astype(BF16)
    part = jnp.dot(x_hi, wsplit_ref[...], preferred_element_type=F32)
    logits = (part[:, :LANES] + part[:, LANES:]
              + jnp.dot(x_lo, wsplit_ref[:, :LANES], preferred_element_type=F32)) + br_ref[...]

    lane_i = _lane_iota((tm, LANES))
    lane = lane_i.astype(F32)
    big = 1e6
    is_g = (lane_i >= N_EXPERTS) & (lane_i < N_EXPERTS + N_GROUPS)
    gl = jnp.where(is_g, logits, NEG_BIG)
    gmax = jnp.max(gl, axis=-1, keepdims=True)
    grp = jnp.min(jnp.where(gl == gmax, lane, big), axis=-1, keepdims=True) - N_EXPERTS
    p_g = 1.0 / jnp.sum(jnp.where(is_g, jnp.exp(gl - gmax), 0.0), axis=-1, keepdims=True)

    in_grp = (lane_i // EXPERTS_PER_GROUP).astype(F32) == grp
    el = jnp.where(in_grp, logits, NEG_BIG)
    emax = jnp.max(el, axis=-1, keepdims=True)
    pe = jnp.where(in_grp, jnp.exp(el - emax), 0.0)
    pe = pe / jnp.sum(pe, axis=-1, keepdims=True)
    pm = jnp.where(in_grp, pe, -1.0)
    p1 = jnp.max(pm, axis=-1, keepdims=True)
    i1 = jnp.min(jnp.where(pm == p1, lane, big), axis=-1, keepdims=True)
    pm2 = jnp.where(lane == i1, -1.0, pm)
    p2 = jnp.max(pm2, axis=-1, keepdims=True)
    i2 = jnp.min(jnp.where(pm2 == p2, lane, big), axis=-1, keepdims=True)
    den = p1 + p2
    g1 = p_g * p1 / den
    g2 = p_g * p2 / den

    hit1 = lane == i1
    hit2 = lane == i2
    oh1 = jnp.where(hit1, 1.0, 0.0)
    oh2 = jnp.where(hit2, 1.0, 0.0)
    row = lax.broadcasted_iota(I32, (tm, tm), 0)
    col = lax.broadcasted_iota(I32, (tm, tm), 1)
    tri = jnp.where(col < row, 1.0, 0.0).astype(BF16)
    cum1 = jnp.dot(tri, oh1.astype(BF16), preferred_element_type=F32)
    cum2 = jnp.dot(tri, oh2.astype(BF16), preferred_element_type=F32)
    tot1 = jnp.sum(oh1, axis=0, keepdims=True)
    tot2 = jnp.sum(oh2, axis=0, keepdims=True)
    carry = carry_ref[...]
    r1 = jnp.sum(jnp.where(hit1, carry + cum1, 0.0), axis=-1, keepdims=True)
    r2 = jnp.sum(jnp.where(hit2, carry + tot1 + cum2, 0.0), axis=-1, keepdims=True)
    new_carry = carry + tot1 + tot2
    carry_ref[...] = new_carry
    cnt_ref[...] = new_carry

    zero = jnp.zeros((tm, LANES), F32)
    route = jnp.where(lane_i == 0, i1, jnp.where(lane_i == 1, i2,
            jnp.where(lane_i == 2, r1, jnp.where(lane_i == 3, r2,
            jnp.where(lane_i == 4, g1, jnp.where(lane_i == 5, g2, zero))))))
    route_ref[...] = route


def _oproj_router(o, gate, h, wo, gffn, wr, br, *, tm):
    tokens, d = h.shape
    gated = gate is not None
    row_spec = pl.BlockSpec((tm, d), lambda i: (i, 0))
    const = lambda shape: pl.BlockSpec(shape, lambda i: (0, 0))
    in_specs = [row_spec] + ([row_spec] if gated else []) + [
        row_spec, const(wo.shape), const((1, d)), const(wr.shape), const((1, LANES))]
    args = [o] + ([gate] if gated else []) + [h, wo, gffn, wr, br]
    return pl.pallas_call(
        functools.partial(_oproj_router_kernel, tm=tm, gated=gated),
        grid=(tokens // tm,),
        in_specs=in_specs,
        out_specs=[row_spec, row_spec, pl.BlockSpec((tm, LANES), lambda i: (i, 0)), const((1, LANES))],
        out_shape=[jax.ShapeDtypeStruct((tokens, d), F32), jax.ShapeDtypeStruct((tokens, d), F32),
                   jax.ShapeDtypeStruct((tokens, LANES), F32), jax.ShapeDtypeStruct((1, LANES), F32)],
        scratch_shapes=[pltpu.VMEM((1, LANES), F32), pltpu.VMEM((d, 2 * LANES), BF16)],
        compiler_params=_params(("arbitrary",)),
        name="oproj_router",
    )(*args)


def _plan_kernel(cnt_ref, plan_ref, blk_ref, *, n_blocks):
    cnt = cnt_ref[...]
    nblk = jnp.floor((cnt + (MOE_ROWS - 1)) * (1.0 / MOE_ROWS))
    row = lax.broadcasted_iota(I32, (LANES, LANES), 0)
    col = lax.broadcasted_iota(I32, (LANES, LANES), 1)
    upper = jnp.where(row < col, 1.0, 0.0).astype(BF16)
    nb8 = jnp.broadcast_to(nblk, (8, LANES)).astype(BF16)
    bstart = jnp.dot(nb8, upper, preferred_element_type=F32)[0:1, :]
    bend = bstart + nblk
    lane = _lane_iota((1, LANES))
    total = jnp.sum(nblk, axis=-1, keepdims=True)
    sub = lax.broadcasted_iota(I32, (8, LANES), 0)
    plan = jnp.where(sub == 0, jnp.broadcast_to(bstart, (8, LANES)),
           jnp.where(sub == 1, jnp.broadcast_to(nblk, (8, LANES)),
           jnp.where(sub == 2, jnp.broadcast_to(total, (8, LANES)), 0.0)))
    plan_ref[...] = plan
    bidx = lax.broadcasted_iota(I32, (n_blocks, LANES), 0).astype(F32)
    done = jnp.where((_lane_iota((n_blocks, LANES)) < N_EXPERTS) & (bend <= bidx), 1.0, 0.0)
    last_used = jnp.max(jnp.where(nblk > 0.0, lane.astype(F32), 0.0), axis=-1, keepdims=True)
    e_of_b = jnp.minimum(jnp.sum(done, axis=-1, keepdims=True), last_used)
    blk_ref[...] = jnp.broadcast_to(e_of_b, (n_blocks, LANES))


def _dest_kernel(route_ref, plan_ref, dest_ref, *, tm):
    route = route_ref[...]
    bstart = plan_ref[0:1, :] * float(MOE_ROWS)
    lane = _lane_iota((tm, LANES)).astype(F32)
    e1 = route[:, 0:1]
    e2 = route[:, 1:2]
    d1 =jnp.sum(jnp.where(lane == e1, bstart, 0.0), axis=-1, keepdims=True) + route[:, 2:3]
    d2 = jnp.sum(jnp.where(lane == e2, bstart, 0.0), axis=-1, keepdims=True) + route[:, 3:4]
    dest_ref[...] = jnp.where(lane == 0, d1, jnp.where(lane == 1, d2, 0.0))


def _dispatch_kernel(dest_ref, bstart_ref, nblk_ref, nused_ref, x_ref, buf_hbm, zeros_ref, sem_zero, sem_rows,
                     *, td, n_blocks):
    step = pl.program_id(0)

    @pl.when(step == 0)
    def _():
        zeros_ref[...] = jnp.zeros_like(zeros_ref)

        def block_copy(blk):
            return pltpu.make_async_copy(
                zeros_ref, buf_hbm.at[pl.ds(pl.multiple_of(blk * MOE_ROWS, MOE_ROWS), MOE_ROWS), :], sem_zero)

        def zero_copy(e):
            return block_copy(bstart_ref[e] + nblk_ref[e] - 1)

        def start_tail(blk, c):
            block_copy(blk).start()
            return c

        def wait_tail(blk, c):
            block_copy(blk).wait()
            return c

        lax.fori_loop(nused_ref[0], n_blocks, start_tail, 0)
        lax.fori_loop(nused_ref[0], n_blocks, wait_tail, 0)

        def start(e, c):
            @pl.when(nblk_ref[e] > 0)
            def _():
                zero_copy(e).start()
            return c

        def wait(e, c):
            @pl.when(nblk_ref[e] > 0)
            def _():
                zero_copy(e).wait()
            return c

        lax.fori_loop(0, N_EXPERTS, start, 0)
        lax.fori_loop(0, N_EXPERTS, wait, 0)

    def start_token(t, c):
        for k in range(2):
            pltpu.make_async_copy(x_ref.at[pl.ds(t, 1), :], buf_hbm.at[pl.ds(dest_ref[2 * t + k], 1), :],
                                  sem_rows).start()
        return c

    lax.fori_loop(0, td, start_token, 0, unroll=4)
    for _ in range(2):
        pltpu.make_async_copy(x_ref, buf_hbm.at[pl.ds(0, td), :], sem_rows).wait()


def _expert_kernel(blk_ref, nused_ref, bstart_ref, nblk_ref, x_ref, wgu_hbm, wdn_hbm, y_ref,
                   wgu_f32, wdn_f32, wgu_bf, wdn_bf, slot_ref, sems, *, d_expert):
    b = pl.program_id(0)

    def weight_copies(e, slot):
        return (pltpu.make_async_copy(wgu_hbm.at[e], wgu_f32.at[slot], sems.at[0, slot]),
                pltpu.make_async_copy(wdn_hbm.at[e], wdn_f32.at[slot], sems.at[1, slot]))

    @pl.when(b == 0)
    def _():
        slot_ref[0] = 0
        for c in weight_copies(blk_ref[0], 0):
            c.start(priority=1)

    @pl.when(b < nused_ref[0])
    def _():
        e = blk_ref[b]
        first = (b == 0) | (e != blk_ref[jnp.maximum(b - 1, 0)])

        @pl.when(first)
        def _():
            slot = slot_ref[0]
            nxt = bstart_ref[e] + nblk_ref[e]

            @pl.when(nxt < nused_ref[0])
            def _():
                for c in weight_copies(blk_ref[nxt], 1 - slot):
                    c.start(priority=1)

            for c in weight_copies(e, slot):
                c.wait()
            wgu_bf[...] = wgu_f32[slot].astype(BF16)
            wdn_bf[...] = wdn_f32[slot].astype(BF16)
            slot_ref[0] = 1 - slot

        x = x_ref[...].astype(BF16)
        gu = jnp.dot(x, wgu_bf[...], preferred_element_type=F32)
        g = gu[:, :d_expert]
        act = (g * jax.nn.sigmoid(g)) * gu[:, d_expert:]
        y_ref[...] = jnp.dot(act.astype(BF16), wdn_bf[...], preferred_element_type=F32)

    @pl.when(b >= nused_ref[0])
    def _():
        y_ref[...] = jnp.zeros_like(y_ref)


def _combine_kernel(dest_ref, dest_next_ref, y_hbm, route_ref, h_ref, g_ref, out_ref, xn_ref, rows_ref, sems,
                    *, tc, with_norm):
    step = pl.program_id(0)
    nsteps = pl.num_programs(0)
    slot = step % 2

    def issue(idx_ref, to_slot):
        def start_token(t, c):
            for k in range(2):
                pltpu.make_async_copy(y_hbm.at[pl.ds(idx_ref[2 * t + k], 1), :],
                                      rows_ref.at[to_slot, k, pl.ds(t, 1), :], sems.at[to_slot]).start()
            return c
        lax.fori_loop(0, tc, start_token, 0, unroll=4)

    @pl.when(step == 0)
    def _():
        issue(dest_ref, 0)

    @pl.when(step + 1 < nsteps)
    def _():
        issue(dest_next_ref, 1 - slot)

    for k in range(2):
        pltpu.make_async_copy(y_hbm.at[pl.ds(0, tc), :], rows_ref.at[slot, k], sems.at[slot]).wait()
    route = route_ref[...]
    out = h_ref[...] + (rows_ref[slot, 0] * route[:, 4:5] + rows_ref[slot, 1] * route[:, 5:6])
    out_ref[...] = out
    if with_norm:
        xn_ref[...] = _rms(out, g_ref[...]).astype(BF16)
    else:
        xn_ref[...] = jnp.zeros_like(xn_ref)


def _moe(xn, route, cnt, h1, w_gate_up, w_down, next_gain):
    tokens, d = xn.shape
    d_expert = w_down.shape[1]
    n_blocks = (2 * tokens + N_EXPERTS * (MOE_ROWS - 1) + MOE_ROWS - 1) // MOE_ROWS
    assert n_blocks <= 256, "block counts must stay exact in bf16"
    nb_pad = (n_blocks + 7) // 8 * 8

    plan, blk = pl.pallas_call(
        functools.partial(_plan_kernel, n_blocks=nb_pad),
        out_shape=[jax.ShapeDtypeStruct((8, LANES), F32), jax.ShapeDtypeStruct((nb_pad, LANES), F32)],
        name="moe_plan",
    )(cnt)

    tm = 512
    dest = pl.pallas_call(
        functools.partial(_dest_kernel, tm=tm),
        grid=(tokens // tm,),
        in_specs=[pl.BlockSpec((tm, LANES), lambda i: (i, 0)), pl.BlockSpec((8, LANES), lambda i: (0, 0))],
        out_specs=pl.BlockSpec((tm, LANES), lambda i: (i, 0)),
        out_shape=jax.ShapeDtypeStruct((tokens, LANES), F32),
        compiler_params=_params(("parallel",)),
        name="moe_dest",
    )(route, plan)

    dest_flat = dest[:, :2].astype(I32).reshape(-1)
    bstart = plan[0, :N_EXPERTS].astype(I32)
    nblk = plan[1, :N_EXPERTS].astype(I32)
    n_used = plan[2, :1].astype(I32)
    blk_e = blk[:n_blocks, 0].astype(I32)

    td = 1024
    smem = pltpu.SMEM
    buf = pl.pallas_call(
        functools.partial(_dispatch_kernel, td=td, n_blocks=n_blocks),
        grid=(tokens // td,),
        in_specs=[
            pl.BlockSpec((2 * td,), lambda i: (i,), memory_space=smem),
            pl.BlockSpec((N_EXPERTS,), lambda i: (0,), memory_space=smem),
            pl.BlockSpec((N_EXPERTS,), lambda i: (0,), memory_space=smem),
            pl.BlockSpec((1,), lambda i: (0,), memory_space=smem),
            pl.BlockSpec((td, d), lambda i: (i, 0)),
        ],
        out_specs=pl.BlockSpec(memory_space=pl.ANY),
        out_shape=jax.ShapeDtypeStruct((n_blocks * MOE_ROWS, d), F32),
        scratch_shapes=[pltpu.VMEM((MOE_ROWS, d), F32), pltpu.SemaphoreType.DMA(()), pltpu.SemaphoreType.DMA(())],
        compiler_params=_params(("arbitrary",)),
        name="moe_dispatch",
    )(dest_flat, bstart, nblk, n_used, xn)

    y = pl.pallas_call(
        functools.partial(_expert_kernel, d_expert=d_expert),
        grid_spec=pltpu.PrefetchScalarGridSpec(
            num_scalar_prefetch=4,
            grid=(n_blocks,),
            in_specs=[
                pl.BlockSpec((MOE_ROWS, d), lambda b, *_: (b, 0)),
                pl.BlockSpec(memory_space=pl.ANY),
                pl.BlockSpec(memory_space=pl.ANY),
            ],
            out_specs=pl.BlockSpec((MOE_ROWS, d), lambda b, *_: (b, 0)),
            scratch_shapes=[pltpu.VMEM((2, d, 2 * d_expert), F32), pltpu.VMEM((2, d_expert, d), F32),
                            pltpu.VMEM((d, 2 * d_expert), BF16), pltpu.VMEM((d_expert, d), BF16),
                            pltpu.SMEM((1,), I32), pltpu.SemaphoreType.DMA((2, 2))],
        ),
        out_shape=jax.ShapeDtypeStruct((n_blocks * MOE_ROWS, d), F32),
        compiler_params=_params(("arbitrary",)),
        name="moe_experts",
    )(blk_e, n_used, bstart, nblk, buf, w_gate_up, w_down)

    tc = 512
    with_norm = next_gain is not None
    gain = next_gain if with_norm else jnp.ones((1, d), F32)
    row_spec = pl.BlockSpec((tc, d), lambda i: (i, 0))
    out, xn_next = pl.pallas_call(
        functools.partial(_combine_kernel, tc=tc, with_norm=with_norm),
        grid=(tokens // tc,),
        in_specs=[
            pl.BlockSpec((2 * tc,), lambda i: (i,), memory_space=smem),
            pl.BlockSpec((2 * tc,), lambda i: (jnp.minimum(i + 1, tokens // tc - 1),), memory_space=smem),
            pl.BlockSpec(memory_space=pl.ANY),
            pl.BlockSpec((tc, LANES), lambda i: (i, 0)),
            row_spec,
            pl.BlockSpec((1, d), lambda i: (0, 0)),
        ],
        out_specs=[row_spec, row_spec if with_norm else pl.BlockSpec((8, LANES), lambda i: (0, 0))],
        out_shape=[jax.ShapeDtypeStruct((tokens, d), F32),
                   jax.ShapeDtypeStruct((tokens, d), BF16) if with_norm else jax.ShapeDtypeStruct((8, LANES), BF16)],
        scratch_shapes=[pltpu.VMEM((2, 2, tc, d), F32), pltpu.SemaphoreType.DMA((2,))],
        compiler_params=_params(("arbitrary",)),
        name="moe_combine",
    )(dest_flat, dest_flat, y, route, h1, gain)
    return out, (xn_next if with_norm else None)


def _router_weights(w_group, b_group, w_expert, b_expert):
    d = w_group.shape[0]
    pad = LANES - N_EXPERTS - N_GROUPS
    w = jnp.concatenate([w_expert, w_group, jnp.zeros((d, pad), F32)], axis=1)
    b = jnp.concatenate([b_expert, b_group, jnp.zeros((pad,), F32)])[None, :]
    return w, b


def _row(v):
    return v.astype(F32)[None, :]


def kernel(x, positions, l0_norm_mix, l0_mla_w_in, l0_mla_q_lat_norm, l0_mla_kv_lat_norm, l0_mla_w_uq, l0_mla_w_ukv, l0_mla_qk_gain, l0_mla_w_o, l0_norm_ffn, l0_router_group, l0_router_group_bias, l0_router_expert, l0_router_expert_bias, l0_w_gate_up, l0_w_down, l1_norm_mix, l1_fox_w_in, l1_fox_forget_bias, l1_fox_qk_gain, l1_fox_w_o, l1_norm_ffn, l1_router_group, l1_router_group_bias, l1_router_expert, l1_router_expert_bias, l1_w_gate_up, l1_w_down):
    batch, seq, d = x.shape
    tokens = batch * seq
    q_lora = l0_mla_q_lat_norm.shape[0]
    kv_lora = l0_mla_kv_lat_norm.shape[0]
    rope = l0_mla_w_in.shape[1] - q_lora - kv_lora
    nope = l0_mla_qk_gain.shape[1] - rope
    heads = l0_mla_w_uq.shape[1] // (nope + rope)
    assert rope == 64 and nope == LANES and heads % 2 == 0
    tk_attn = min(512, seq // 2)

    h0 = x.reshape(tokens, d)
    pos = positions.reshape(tokens, 1)

    half = rope // 2
    freqs = ROPE_THETA ** (-jnp.arange(half, dtype=F32) / half)
    freq_row = jnp.tile(freqs, LANES // half)[None, :]
    gkr = _row(l0_mla_qk_gain[1, nope:])
    tm = 512
    row = lambda w: pl.BlockSpec((tm, w), lambda i: (i, 0))
    const = lambda shape: pl.BlockSpec(shape, lambda i: (0, 0))
    win = l0_mla_w_in.astype(BF16)
    cq, ckv, kr2, cos_t, sin_t = pl.pallas_call(
        functools.partial(_mla_in_kernel, q_lora=q_lora, kv_lora=kv_lora),
        grid=(tokens // tm,),
        in_specs=[row(d), row(1), const((1, d)), const(win.shape), const((1, q_lora)), const((1, kv_lora)),
                  const((1, rope)), const((1, LANES))],
        out_specs=[row(q_lora), row(kv_lora), row(LANES), row(LANES), row(LANES)],
        out_shape=[jax.ShapeDtypeStruct((tokens, q_lora), BF16), jax.ShapeDtypeStruct((tokens, kv_lora), BF16),
                   jax.ShapeDtypeStruct((tokens, LANES), BF16), jax.ShapeDtypeStruct((tokens, LANES), F32),
                   jax.ShapeDtypeStruct((tokens, LANES), F32)],
        compiler_params=_params(("parallel",)),
        name="mla_in",
    )(h0, pos, _row(l0_norm_mix), win, _row(l0_mla_q_lat_norm), _row(l0_mla_kv_lat_norm), gkr, freq_row)

    wuq = l0_mla_w_uq.reshape(q_lora, heads, nope + rope)
    wuq = jnp.concatenate([wuq[:, :, :nope].reshape(q_lora, heads * nope),
                           wuq[:, :, nope:].reshape(q_lora, heads * rope)], axis=1).astype(BF16)
    scale = LOG2E / math.sqrt(nope + rope)
    q_aug = pl.pallas_call(
        functools.partial(_mla_q_kernel, heads=heads, scale=scale),
        grid=(tokens // tm,),
        in_specs=[row(q_lora), const(wuq.shape), const((1, nope)), const((1, LANES)), row(LANES), row(LANES)],
        out_specs=row(heads * HEAD_PAD),
        out_shape=jax.ShapeDtypeStruct((tokens, heads * HEAD_PAD), BF16),
        compiler_params=_params(("parallel",)),
        name="mla_q",
    )(cq, wuq, _row(l0_mla_qk_gain[0, :nope]), _row(jnp.tile(l0_mla_qk_gain[0, nope:], 2)), cos_t, sin_t)

    v_dim = l0_mla_w_ukv.shape[1] // heads - nope
    assert v_dim == LANES
    wukv = l0_mla_w_ukv.reshape(kv_lora, heads, nope + v_dim)
    wukv = jnp.concatenate([wukv[:, :, :nope].reshape(kv_lora, heads * nope),
                            wukv[:, :, nope:].reshape(kv_lora, heads * v_dim)], axis=1).astype(BF16)
    k_aug, v0 = pl.pallas_call(
        functools.partial(_mla_kv_kernel, heads=heads),
        grid=(tokens // tm,),
        in_specs=[row(kv_lora), const(wukv.shape), const((1, nope)), row(LANES)],
        out_specs=[row(heads * HEAD_PAD), row(heads * v_dim)],
        out_shape=[jax.ShapeDtypeStruct((tokens, heads * HEAD_PAD), BF16),
                   jax.ShapeDtypeStruct((tokens, heads * v_dim), BF16)],
        compiler_params=_params(("parallel",)),
        name="mla_kv",
    )(ckv, wukv, _row(l0_mla_qk_gain[1, :nope]), kr2)

    o0 = _attention(q_aug, k_aug, v0, batch=batch, seq=seq, heads=heads, tk=tk_attn)

    wr0, br0 = _router_weights(l0_router_group, l0_router_group_bias, l0_router_expert, l0_router_expert_bias)
    h1, xn1, route0, cnt0 = _oproj_router(o0, None, h0, l0_mla_w_o.astype(BF16), _row(l0_norm_ffn), wr0, br0, tm=512)
    h2, xn2 = _moe(xn1, route0, cnt0, h1, l0_w_gate_up, l0_w_down, _row(l1_norm_mix))

    fheads = l1_fox_forget_bias.shape[0]
    hd = l1_fox_qk_gain.shape[1]
    assert hd == LANES
    hh = fheads * hd
    w1 = l1_fox_w_in
    wq1, wk1, wv1 = (w1[:, i * hh:(i + 1) * hh].astype(BF16) for i in range(3))
    wf1 = jnp.concatenate([w1[:, 3 * hh:3 * hh + fheads], jnp.zeros((d, LANES - fheads), F32)], axis=1).astype(BF16)
    wg1 = w1[:, 3 * hh + fheads:].astype(BF16)
    fbias = jnp.concatenate([l1_fox_forget_bias.astype(F32), jnp.zeros((LANES - fheads,), F32)])[None, :]

    ts = min(512, seq)
    ns = seq // ts
    cdec = pl.pallas_call(
        functools.partial(_fox_decay_kernel, ts=ts),
        grid=(batch, ns),
        in_specs=[pl.BlockSpec((ts, d), lambda b, i: (b * ns + i, 0)), pl.BlockSpec((d, LANES), lambda b, i: (0, 0)),
                  pl.BlockSpec((1, LANES), lambda b, i: (0, 0))],
        out_specs=pl.BlockSpec((ts, LANES), lambda b, i: (b * ns + i, 0)),
        out_shape=jax.ShapeDtypeStruct((tokens, LANES), F32),
        scratch_shapes=[pltpu.VMEM((1, LANES), F32)],
        compiler_params=_params(("arbitrary", "arbitrary")),
        name="fox_decay",
    )(xn2, wf1, fbias)

    def fox_proj(w, gain, mode, out_width, out_dtype):
        return pl.pallas_call(
            functools.partial(_fox_proj_kernel, heads=fheads, mode=mode,
                              scale=(LOG2E / math.sqrt(hd)) if mode == "q" else 1.0),
            grid=(tokens // tm,),
            in_specs=[row(d), const(w.shape), const((1, hd)), row(LANES)],
            out_specs=row(out_width),
            out_shape=jax.ShapeDtypeStruct((tokens, out_width), out_dtype),
            compiler_params=_params(("parallel",)),
            name="fox_proj_" + mode,
        )(xn2, w, gain, cdec)

    q1 = fox_proj(wq1, _row(l1_fox_qk_gain[0]), "q", fheads * HEAD_PAD, BF16)
    k1 = fox_proj(wk1, _row(l1_fox_qk_gain[1]), "k", fheads * HEAD_PAD, BF16)
    v1 = fox_proj(wv1, _row(l1_fox_qk_gain[0]), "v", hh, BF16)
    gate1 = fox_proj(wg1, _row(l1_fox_qk_gain[0]), "gate", hh, F32)

    o1 = _attention(q1, k1, v1, batch=batch, seq=seq, heads=fheads, tk=tk_attn)

    wr1, br1 = _router_weights(l1_router_group, l1_router_group_bias, l1_router_expert, l1_router_expert_bias)
    h3, xn3, route1, cnt1 = _oproj_router(o1, gate1, h2, l1_fox_w_o.astype(BF16), _row(l1_norm_ffn), wr1, br1, tm=512)
    h4, _ = _moe(xn3, route1, cnt1, h3, l1_w_gate_up, l1_w_down, None)
    return h4.reshape(batch, seq, d)
```

```python
import functools
import math

import jax
import jax.numpy as jnp
from jax import lax
from jax.experimental import pallas as pl
from jax.experimental.pallas import tpu as pltpu

F32 = jnp.float32
BF16 = jnp.bfloat16
I32 = jnp.int32

EPS = 1e-6
ROPE_THETA = 10000.0
LANES = 128
HEAD_PAD = 256
N_GROUPS = 8
EXPERTS_PER_GROUP = 8
N_EXPERTS = N_GROUPS * EXPERTS_PER_GROUP
MOE_ROWS = 256
NEG_BIG = -1e30
LOG2E = math.log2(math.e)
VMEM_LIMIT = 52 * 1024 * 1024


def _params(sem, vmem=VMEM_LIMIT):
    return pltpu.CompilerParams(dimension_semantics=sem, vmem_limit_bytes=vmem)


def _rms(x, gain):
    ms = jnp.mean(x * x, axis=-1, keepdims=True)
    return x * lax.rsqrt(ms + EPS) * gain


def _lane_iota(shape):
    return lax.broadcasted_iota(I32, shape, len(shape) - 1)


def _rope_slab(x, cos, sin_signed):
    lane = _lane_iota((1, LANES))
    low = (lane % 64) < 32
    swapped = jnp.where(low, pltpu.roll(x, 96, 1), pltpu.roll(x, 32, 1))
    return x * cos + swapped * sin_signed


def _split3(x):
    a = x.astype(BF16)
    r = x - a.astype(F32)
    b = r.astype(BF16)
    c = (r - b.astype(F32)).astype(BF16)
    return a, b, c


def _mla_in_kernel(h_ref, pos_ref, gmix_ref, win_ref, gq_ref, gkv_ref, gkr_ref, freq_ref,
                   cq_ref, ckv_ref, kr_ref, cos_ref, sin_ref, *, q_lora, kv_lora):
    xn = _rms(h_ref[...], gmix_ref[...]).astype(BF16)
    z = jnp.dot(xn, win_ref[...], preferred_element_type=F32)
    cq_ref[...] = _rms(z[:, :q_lora], gq_ref[...]).astype(BF16)
    ckv_ref[...] = _rms(z[:, q_lora:q_lora + kv_lora], gkv_ref[...]).astype(BF16)
    kr = _rms(z[:, q_lora + kv_lora:], gkr_ref[...])
    slab = jnp.concatenate([kr, kr], axis=-1)
    ang = pos_ref[...].astype(F32) * freq_ref[...]
    lane = _lane_iota((1, LANES))
    cos = jnp.cos(ang)
    sin_signed = jnp.where((lane % 64) < 32, -jnp.sin(ang), jnp.sin(ang))
    cos_ref[...] = cos
    sin_ref[...] = sin_signed
    kr_ref[...] = _rope_slab(slab, cos, sin_signed).astype(BF16)


def _mla_q_kernel(cq_ref, w_ref, gn_ref, gr_ref, cos_ref, sin_ref, q_ref, *, heads, scale):
    q = jnp.dot(cq_ref[...], w_ref[...], preferred_element_type=F32)
    lane = _lane_iota((1, LANES))
    low = lane < 64
    cos = cos_ref[...]
    sin_signed = sin_ref[...]
    for h in range(heads):
        n = _rms(q[:, h * LANES:(h + 1) * LANES], gn_ref[...]) * scale
        q_ref[:, h * HEAD_PAD:h * HEAD_PAD + LANES] = n.astype(BF16)
    base = heads * LANES
    for p in range(heads // 2):
        slab = q[:, base + p * LANES:base + (p + 1) * LANES]
        sq = slab * slab
        ss0 = jnp.sum(jnp.where(low, sq, 0.0), axis=-1, keepdims=True)
        ss1 = jnp.sum(jnp.where(low, 0.0, sq), axis=-1, keepdims=True)
        inv = jnp.where(low, lax.rsqrt(ss0 / 64.0 + EPS), lax.rsqrt(ss1 / 64.0 + EPS))
        r = _rope_slab(slab * inv * gr_ref[...], cos, sin_signed) * scale
        zero = jnp.zeros_like(r)
        q_ref[:, (2 * p) * HEAD_PAD + LANES:(2 * p + 1) * HEAD_PAD] = jnp.where(low, r, zero).astype(BF16)
        q_ref[:, (2 * p + 1) * HEAD_PAD + LANES:(2 * p + 2) * HEAD_PAD] = jnp.where(low, zero, r).astype(BF16)


def _mla_kv_kernel(ckv_ref, w_ref, gn_ref, kr_ref, k_ref, v_ref, *, heads):
    kv = jnp.dot(ckv_ref[...], w_ref[...], preferred_element_type=F32)
    lane = _lane_iota((1, LANES))
    low = lane < 64
    kr = kr_ref[...]
    zero = jnp.zeros_like(kr)
    kr_even = jnp.where(low, kr, zero)
    kr_odd = jnp.where(low, zero, kr)
    for h in range(heads):
        n = _rms(kv[:, h * LANES:(h + 1) * LANES], gn_ref[...])
        k_ref[:, h * HEAD_PAD:h * HEAD_PAD + LANES] = n.astype(BF16)
        k_ref[:, h * HEAD_PAD + LANES:(h + 1) * HEAD_PAD] = kr_even if h % 2 == 0 else kr_odd
    v_ref[...] = kv[:, heads * LANES:].astype(BF16)


def _fox_decay_kernel(xn_ref, w_ref, b_ref, c_ref, carry_ref, *, ts):
    @pl.when(pl.program_id(1) == 0)
    def _():
        carry_ref[...] = jnp.zeros_like(carry_ref)

    f = jnp.dot(xn_ref[...], w_ref[...], preferred_element_type=F32) + b_ref[...]
    log_f = jnp.minimum(f, 0.0) - jnp.log1p(jnp.exp(-jnp.abs(f)))
    row = lax.broadcasted_iota(I32, (ts, ts), 0)
    col = lax.broadcasted_iota(I32, (ts, ts), 1)
    tri = jnp.where(col <= row, 1.0, 0.0).astype(BF16)
    a, b, c = _split3(log_f)
    parts = jnp.dot(tri, jnp.concatenate([a, b, c], axis=-1), preferred_element_type=F32)
    local = parts[:, :LANES] + parts[:, LANES:2 * LANES] + parts[:, 2 * LANES:]
    out = carry_ref[...] + local
    c_ref[...] = out
    carry_ref[...] = out[ts - 1:ts, :]


def _fox_proj_kernel(xn_ref, w_ref, g_ref, c_ref, o_ref, *, heads, mode, scale):
    z = jnp.dot(xn_ref[...], w_ref[...], preferred_element_type=F32)
    if mode == "v":
        o_ref[...] = z.astype(BF16)
        return
    if mode == "gate":
        o_ref[...] = z
        return
    lane = _lane_iota((1, LANES))
    cdec = c_ref[...] * LOG2E
    for h in range(heads):
        n = _rms(z[:, h * LANES:(h + 1) * LANES], g_ref[...]) * scale
        o_ref[:, h * HEAD_PAD:h * HEAD_PAD + LANES] = n.astype(BF16)
        ch = cdec[:, h:h + 1]
        if mode == "k":
            ch = -ch
        a, b, c = (t.astype(F32) for t in _split3(ch))
        if mode == "q":
            aug = jnp.where(lane == 0, a, jnp.where(lane == 1, b, jnp.where(lane == 2, c,
                  jnp.where(lane < 6, 1.0, 0.0))))
        else:
            aug = jnp.where(lane < 3, 1.0, jnp.where(lane == 3, a, jnp.where(lane == 4, b,
                  jnp.where(lane == 5, c, 0.0))))
        o_ref[:, h * HEAD_PAD + LANES:(h + 1) * HEAD_PAD] = aug.astype(BF16)


def _attn_tile(q, k_ref, v_ref, j, m_ref, l_ref, acc_ref, *, tk, masked):
    rows = q.shape[0]
    start = pl.multiple_of(j * tk, tk)
    k = k_ref[pl.ds(start, tk), :]
    v = v_ref[pl.ds(start, tk), :]
    s = lax.dot_general(q, k, (((1,), (1,)), ((), ())), preferred_element_type=F32)
    if masked:
        row = lax.broadcasted_iota(I32, (rows, tk), 0)
        col = lax.broadcasted_iota(I32, (rows, tk), 1)
        s = jnp.where(row >= col, s, NEG_BIG)
    m_prev = m_ref[...]
    m_new = jnp.maximum(m_prev, jnp.max(s, axis=-1, keepdims=True))
    alpha = jnp.exp2(m_prev - m_new)
    p = jnp.exp2(s - jnp.concatenate([m_new] * (tk // LANES), axis=1))
    psum = p[:, :LANES]
    for c in range(1, tk // LANES):
        psum = psum + p[:, c * LANES:(c + 1) * LANES]
    l_ref[...] = alpha * l_ref[...] + psum
    acc_ref[...] = alpha * acc_ref[...] + jnp.dot(p.astype(BF16), v, preferred_element_type=F32)
    m_ref[...] = m_new


def _attn_kernel(q_ref, k_ref, v_ref, o_ref, m_a, l_a, acc_a, m_b, l_b, acc_b, *, tk):
    qi = pl.program_id(2)
    m_a[...] = jnp.full_like(m_a, NEG_BIG)
    m_b[...] = jnp.full_like(m_b, NEG_BIG)

    @pl.when((pl.program_id(0) == 0) & (pl.program_id(1) == 0) & (qi == 0))
    def _():
        for ref in (l_a, acc_a, l_b, acc_b):
            ref[...] = jnp.zeros_like(ref)

    q = q_ref[...]
    tile = functools.partial(_attn_tile, k_ref=k_ref, v_ref=v_ref, tk=tk)

    def pair(jj):
        tile(q, j=2 * jj, m_ref=m_a, l_ref=l_a, acc_ref=acc_a, masked=False)
        tile(q, j=2 * jj + 1, m_ref=m_b, l_ref=l_b, acc_ref=acc_b, masked=False)

    def body(u, carry):
        pair(2 * u)
        pair(2 * u + 1)
        return carry

    lax.fori_loop(0, qi // 2, body, 0)

    @pl.when(qi % 2 == 1)
    def _():
        pair(qi - 1)

    tile(q, j=2 * qi, m_ref=m_a, l_ref=l_a, acc_ref=acc_a, masked=True)
    low = pl.ds(tk, tk)
    tile(q_ref[low, :], j=2 * qi + 1, m_ref=m_b.at[low, :], l_ref=l_b.at[low, :], acc_ref=acc_b.at[low, :],
         masked=True)
    m = jnp.maximum(m_a[...], m_b[...])
    w_a = jnp.exp2(m_a[...] - m)
    w_b = jnp.exp2(m_b[...] - m)
    l = jnp.sum(w_a * l_a[...] + w_b * l_b[...], axis=-1, keepdims=True)
    o_ref[...] = ((w_a * acc_a[...] + w_b * acc_b[...]) / l).astype(BF16)


def _attention(q, k, v, *, batch, seq, heads, tk):
    tq = 2 * tk
    nq = seq // tq
    stat = pltpu.VMEM((tq, LANES), F32)
    return pl.pallas_call(
        functools.partial(_attn_kernel, tk=tk),
        grid=(batch, heads, nq),
        in_specs=[
            pl.BlockSpec((tq, HEAD_PAD), lambda b, h, i: (b * nq + i, h)),
            pl.BlockSpec((seq, HEAD_PAD), lambda b, h, i: (b, h)),
            pl.BlockSpec((seq, LANES), lambda b, h, i: (b, h)),
        ],
        out_specs=pl.BlockSpec((tq, LANES), lambda b, h, i: (b * nq + i, h)),
        out_shape=jax.ShapeDtypeStruct((batch * seq, heads * LANES), BF16),
        scratch_shapes=[stat] * 6,
        compiler_params=_params(("arbitrary", "arbitrary", "arbitrary")),
        name="causal_attention",
    )(q, k, v)


def _oproj_router_kernel(*refs, tm, gated):
    if gated:
        (o_ref, gate_ref, h_ref, wo_ref, gffn_ref, wr_ref, br_ref,
         h1_ref, xn_ref, route_ref, cnt_ref, carry_ref, wsplit_ref) = refs
    else:
        (o_ref, h_ref, wo_ref, gffn_ref, wr_ref, br_ref,
         h1_ref, xn_ref, route_ref, cnt_ref, carry_ref, wsplit_ref) = refs

    @pl.when(pl.program_id(0) == 0)
    def _():
        carry_ref[...] = jnp.zeros_like(carry_ref)
        w = wr_ref[...]
        w_hi = w.astype(BF16)
        wsplit_ref[:, :LANES] = w_hi
        wsplit_ref[:, LANES:] = (w - w_hi.astype(F32)).astype(BF16)

    o = o_ref[...]
    if gated:
        o = (o.astype(F32) * jax.nn.sigmoid(gate_ref[...])).astype(BF16)
    h1 = h_ref[...] + jnp.dot(o, wo_ref[...], preferred_element_type=F32)
    h1_ref[...] = h1
    xn = _rms(h1, gffn_ref[...])
    xn_ref[...] = xn

    x_hi = xn.astype(BF16)
    x_lo = (xn - x_hi.astype(F32)).astype(BF16)
    part = jnp.dot(x_hi, wsplit_ref[...], preferred_element_type=F32)
    logits = (part[:, :LANES] + part[:, LANES:]
              + jnp.dot(x_lo, wsplit_ref[:, :LANES], preferred_element_type=F32)) + br_ref[...]

    lane_i = _lane_iota((tm, LANES))
    lane = lane_i.astype(F32)
    big = 1e6
    is_g = (lane_i >= N_EXPERTS) & (lane_i < N_EXPERTS + N_GROUPS)
    gl = jnp.where(is_g, logits, NEG_BIG)
    gmax = jnp.max(gl, axis=-1, keepdims=True)
    grp = jnp.min(jnp.where(gl == gmax, lane, big), axis=-1, keepdims=True) - N_EXPERTS
    p_g = 1.0 / jnp.sum(jnp.where(is_g, jnp.exp(gl - gmax), 0.0), axis=-1, keepdims=True)

    in_grp = (lane_i // EXPERTS_PER_GROUP).astype(F32) == grp
    el = jnp.where(in_grp, logits, NEG_BIG)
    emax = jnp.max(el, axis=-1, keepdims=True)
    pe = jnp.where(in_grp, jnp.exp(el - emax), 0.0)
    pe = pe / jnp.sum(pe, axis=-1, keepdims=True)
    pm = jnp.where(in_grp, pe, -1.0)
    p1 = jnp.max(pm, axis=-1, keepdims=True)
    i1 = jnp.min(jnp.where(pm == p1, lane, big), axis=-1, keepdims=True)
    pm2 = jnp.where(lane == i1, -1.0, pm)
    p2 = jnp.max(pm2, axis=-1, keepdims=True)
    i2 = jnp.min(jnp.where(pm2 == p2, lane, big), axis=-1, keepdims=True)
    den = p1 + p2
    g1 = p_g * p1 / den
    g2 = p_g * p2 / den

    hit1 = lane == i1
    hit2 = lane == i2
    oh1 = jnp.where(hit1, 1.0, 0.0)
    oh2 = jnp.where(hit2, 1.0, 0.0)
    row = lax.broadcasted_iota(I32, (tm, tm), 0)
    col = lax.broadcasted_iota(I32, (tm, tm), 1)
    tri = jnp.where(col < row, 1.0, 0.0).astype(BF16)
    cum1 = jnp.dot(tri, oh1.astype(BF16), preferred_element_type=F32)
    cum2 = jnp.dot(tri, oh2.astype(BF16), preferred_element_type=F32)
    tot1 = jnp.sum(oh1, axis=0, keepdims=True)
    tot2 = jnp.sum(oh2, axis=0, keepdims=True)
    carry = carry_ref[...]
    r1 = jnp.sum(jnp.where(hit1, carry + cum1, 0.0), axis=-1, keepdims=True)
    r2 = jnp.sum(jnp.where(hit2, carry + tot1 + cum2, 0.0), axis=-1, keepdims=True)
    new_carry = carry + tot1 + tot2
    carry_ref[...] = new_carry
    cnt_ref[...] = new_carry

    zero = jnp.zeros((tm, LANES), F32)
    route = jnp.where(lane_i == 0, i1, jnp.where(lane_i == 1, i2,
            jnp.where(lane_i == 2, r1, jnp.where(lane_i == 3, r2,
            jnp.where(lane_i == 4, g1, jnp.where(lane_i == 5, g2, zero))))))
    route_ref[...] = route


def _oproj_router(o, gate, h, wo, gffn, wr, br, *, tm):
    tokens, d = h.shape
    gated = gate is not None
    row_spec = pl.BlockSpec((tm, d), lambda i: (i, 0))
    const = lambda shape: pl.BlockSpec(shape, lambda i: (0, 0))
    in_specs = [row_spec] + ([row_spec] if gated else []) + [
        row_spec, const(wo.shape), const((1, d)), const(wr.shape), const((1, LANES))]
    args = [o] + ([gate] if gated else []) + [h, wo, gffn, wr, br]
    return pl.pallas_call(
        functools.partial(_oproj_router_kernel, tm=tm, gated=gated),
        grid=(tokens // tm,),
        in_specs=in_specs,
        out_specs=[row_spec, row_spec, pl.BlockSpec((tm, LANES), lambda i: (i, 0)), const((1, LANES))],
        out_shape=[jax.ShapeDtypeStruct((tokens, d), F32), jax.ShapeDtypeStruct((tokens, d), F32),
                   jax.ShapeDtypeStruct((tokens, LANES), F32), jax.ShapeDtypeStruct((1, LANES), F32)],
        scratch_shapes=[pltpu.VMEM((1, LANES), F32), pltpu.VMEM((d, 2 * LANES), BF16)],
        compiler_params=_params(("arbitrary",)),
        name="oproj_router",
    )(*args)


def _plan_kernel(cnt_ref, plan_ref, blk_ref, *, n_blocks):
    cnt = cnt_ref[...]
    nblk = jnp.floor((cnt + (MOE_ROWS - 1)) * (1.0 / MOE_ROWS))
    row = lax.broadcasted_iota(I32, (LANES, LANES), 0)
    col = lax.broadcasted_iota(I32, (LANES, LANES), 1)
    upper = jnp.where(row < col, 1.0, 0.0).astype(BF16)
    nb8 = jnp.broadcast_to(nblk, (8, LANES)).astype(BF16)
    bstart = jnp.dot(nb8, upper, preferred_element_type=F32)[0:1, :]
    bend = bstart + nblk
    lane = _lane_iota((1, LANES))
    total = jnp.sum(nblk, axis=-1, keepdims=True)
    sub = lax.broadcasted_iota(I32, (8, LANES), 0)
    plan = jnp.where(sub == 0, jnp.broadcast_to(bstart, (8, LANES)),
           jnp.where(sub == 1, jnp.broadcast_to(nblk, (8, LANES)),
           jnp.where(sub == 2, jnp.broadcast_to(total, (8, LANES)), 0.0)))
    plan_ref[...] = plan
    bidx = lax.broadcasted_iota(I32, (n_blocks, LANES), 0).astype(F32)
    done = jnp.where((_lane_iota((n_blocks, LANES)) < N_EXPERTS) & (bend <= bidx), 1.0, 0.0)
    last_used = jnp.max(jnp.where(nblk > 0.0, lane.astype(F32), 0.0), axis=-1, keepdims=True)
    e_of_b = jnp.minimum(jnp.sum(done, axis=-1, keepdims=True), last_used)
    blk_ref[...] = jnp.broadcast_to(e_of_b, (n_blocks, LANES))


def _dest_kernel(route_ref, plan_ref, dest_ref, *, tm):
    route = route_ref[...]
    bstart = plan_ref[0:1, :] * float(MOE_ROWS)
    lane = _lane_iota((tm, LANES)).astype(F32)
    e1 = route[:, 0:1]
    e2 = route[:, 1:2]
    d1 =jnp.sum(jnp.where(lane == e1, bstart, 0.0), axis=-1, keepdims=True) + route[:, 2:3]
    d2 = jnp.sum(jnp.where(lane == e2, bstart, 0.0), axis=-1, keepdims=True) + route[:, 3:4]
    dest_ref[...] = jnp.where(lane == 0, d1, jnp.where(lane == 1, d2, 0.0))


def _dispatch_kernel(dest_ref, bstart_ref, nblk_ref, cnt_ref, nused_ref, x_ref, buf_hbm, zeros_ref, sem_zero,
                     sem_rows, *, td, n_blocks):
    step = pl.program_id(0)

    def for_each_fill(fn):
        def per_expert(e, c):
            base = bstart_ref[e] * MOE_ROWS + cnt_ref[e]
            head = (8 - (base & 7)) & 7
            for r in range(7):
                @pl.when(r < head)
                def _(r=r):
                    fn(pltpu.make_async_copy(zeros_ref.at[pl.ds(0, 1), :], buf_hbm.at[pl.ds(base + r, 1), :],
                                             sem_zero))
            length = (bstart_ref[e] + nblk_ref[e]) * MOE_ROWS - (base + head)
            off = base + head
            for p in (128, 64, 32, 16, 8):
                @pl.when((length & p) != 0)
                def _(off=off, p=p):
                    fn(pltpu.make_async_copy(zeros_ref.at[pl.ds(0, p), :],
                                             buf_hbm.at[pl.ds(pl.multiple_of(off, 8), p), :], sem_zero))
                off = off + (length & p)
            return c

        lax.fori_loop(0, N_EXPERTS, per_expert, 0)

        def per_tail_block(blk, c):
            fn(pltpu.make_async_copy(
                zeros_ref, buf_hbm.at[pl.ds(pl.multiple_of(blk * MOE_ROWS, MOE_ROWS), MOE_ROWS), :], sem_zero))
            return c

        lax.fori_loop(nused_ref[0], n_blocks, per_tail_block, 0)

    @pl.when(step == 0)
    def _():
        zeros_ref[...] = jnp.zeros_like(zeros_ref)
        for_each_fill(lambda copy: copy.start())

    def start_token(t, c):
        for k in range(2):
            pltpu.make_async_copy(x_ref.at[pl.ds(t, 1), :], buf_hbm.at[pl.ds(dest_ref[2 * t + k], 1), :],
                                  sem_rows).start()
        return c

    lax.fori_loop(0, td, start_token, 0, unroll=4)
    for _ in range(2):
        pltpu.make_async_copy(x_ref, buf_hbm.at[pl.ds(0, td), :], sem_rows).wait()

    @pl.when(step == pl.num_programs(0) - 1)
    def _():
        for_each_fill(lambda copy: copy.wait())


def _expert_kernel(blk_ref, nused_ref, bstart_ref, nblk_ref, x_ref, wgu_hbm, wdn_hbm, y_ref,
                   wgu_f32, wdn_f32, wgu_bf, wdn_bf, slot_ref, sems, *, d_expert):
    b = pl.program_id(0)

    def weight_copies(e, slot):
        return (pltpu.make_async_copy(wgu_hbm.at[e], wgu_f32.at[slot], sems.at[0, slot]),
                pltpu.make_async_copy(wdn_hbm.at[e], wdn_f32.at[slot], sems.at[1, slot]))

    @pl.when(b == 0)
    def _():
        slot_ref[0] = 0
        for c in weight_copies(blk_ref[0], 0):
            c.start(priority=1)

    @pl.when(b < nused_ref[0])
    def _():
        e = blk_ref[b]
        first = (b == 0) | (e != blk_ref[jnp.maximum(b - 1, 0)])

        @pl.when(first)
        def _():
            slot = slot_ref[0]
            nxt = bstart_ref[e] + nblk_ref[e]

            @pl.when(nxt < nused_ref[0])
            def _():
                for c in weight_copies(blk_ref[nxt], 1 - slot):
                    c.start(priority=1)

            for c in weight_copies(e, slot):
                c.wait()
            wgu_bf[...] = wgu_f32[slot].astype(BF16)
            wdn_bf[...] = wdn_f32[slot].astype(BF16)
            slot_ref[0] = 1 - slot

        x = x_ref[...].astype(BF16)
        gu = jnp.dot(x, wgu_bf[...], preferred_element_type=F32)
        g = gu[:, :d_expert]
        act = (g * jax.nn.sigmoid(g)) * gu[:, d_expert:]
        y_ref[...] = jnp.dot(act.astype(BF16), wdn_bf[...], preferred_element_type=F32)

    @pl.when(b >= nused_ref[0])
    def _():
        y_ref[...] = jnp.zeros_like(y_ref)


def _combine_kernel(dest_ref, dest_next_ref, y_hbm, route_ref, h_ref, g_ref, out_ref, xn_ref, rows_ref, sems,
                    *, tc, with_norm):
    step = pl.program_id(0)
    nsteps = pl.num_programs(0)
    slot = step % 2

    def issue(idx_ref, to_slot):
        def start_token(t, c):
            for k in range(2):
                pltpu.make_async_copy(y_hbm.at[pl.ds(idx_ref[2 * t + k], 1), :],
                                      rows_ref.at[to_slot, k, pl.ds(t, 1), :], sems.at[to_slot]).start()
            return c
        lax.fori_loop(0, tc, start_token, 0, unroll=4)

    @pl.when(step == 0)
    def _():
        issue(dest_ref, 0)

    @pl.when(step + 1 < nsteps)
    def _():
        issue(dest_next_ref, 1 - slot)

    for k in range(2):
        pltpu.make_async_copy(y_hbm.at[pl.ds(0, tc), :], rows_ref.at[slot, k], sems.at[slot]).wait()
    route = route_ref[...]
    out = h_ref[...] + (rows_ref[slot, 0] * route[:, 4:5] + rows_ref[slot, 1] * route[:, 5:6])
    out_ref[...] = out
    if with_norm:
        xn_ref[...] = _rms(out, g_ref[...]).astype(BF16)
    else:
        xn_ref[...] = jnp.zeros_like(xn_ref)


def _moe(xn, route, cnt, h1, w_gate_up, w_down, next_gain):
    tokens, d = xn.shape
    d_expert = w_down.shape[1]
    n_blocks = (2 * tokens + N_EXPERTS * (MOE_ROWS - 1) + MOE_ROWS - 1) // MOE_ROWS
    assert n_blocks <= 256, "block counts must stay exact in bf16"
    nb_pad = (n_blocks + 7) // 8 * 8

    plan, blk = pl.pallas_call(
        functools.partial(_plan_kernel, n_blocks=nb_pad),
        out_shape=[jax.ShapeDtypeStruct((8, LANES), F32), jax.ShapeDtypeStruct((nb_pad, LANES), F32)],
        name="moe_plan",
    )(cnt)

    tm = 512
    dest = pl.pallas_call(
        functools.partial(_dest_kernel, tm=tm),
        grid=(tokens // tm,),
        in_specs=[pl.BlockSpec((tm, LANES), lambda i: (i, 0)), pl.BlockSpec((8, LANES), lambda i: (0, 0))],
        out_specs=pl.BlockSpec((tm, LANES), lambda i: (i, 0)),
        out_shape=jax.ShapeDtypeStruct((tokens, LANES), F32),
        compiler_params=_params(("parallel",)),
        name="moe_dest",
    )(route, plan)

    dest_flat = dest[:, :2].astype(I32).reshape(-1)
    bstart = plan[0, :N_EXPERTS].astype(I32)
    nblk = plan[1, :N_EXPERTS].astype(I32)
    n_used = plan[2, :1].astype(I32)
    blk_e = blk[:n_blocks, 0].astype(I32)

    td = 1024
    smem = pltpu.SMEM
    buf = pl.pallas_call(
        functools.partial(_dispatch_kernel, td=td, n_blocks=n_blocks),
        grid=(tokens // td,),
        in_specs=[
            pl.BlockSpec((2 * td,), lambda i: (i,), memory_space=smem),
            pl.BlockSpec((N_EXPERTS,), lambda i: (0,), memory_space=smem),
            pl.BlockSpec((N_EXPERTS,), lambda i: (0,), memory_space=smem),
            pl.BlockSpec((N_EXPERTS,), lambda i: (0,), memory_space=smem),
            pl.BlockSpec((1,), lambda i: (0,), memory_space=smem),
            pl.BlockSpec((td, d), lambda i: (i, 0)),
        ],
        out_specs=pl.BlockSpec(memory_space=pl.ANY),
        out_shape=jax.ShapeDtypeStruct((n_blocks * MOE_ROWS, d), F32),
        scratch_shapes=[pltpu.VMEM((MOE_ROWS, d), F32), pltpu.SemaphoreType.DMA(()), pltpu.SemaphoreType.DMA(())],
        compiler_params=_params(("arbitrary",)),
        name="moe_dispatch",
    )(dest_flat, bstart, nblk, cnt[0, :N_EXPERTS].astype(I32), n_used, xn)

    y = pl.pallas_call(
        functools.partial(_expert_kernel, d_expert=d_expert),
        grid_spec=pltpu.PrefetchScalarGridSpec(
            num_scalar_prefetch=4,
            grid=(n_blocks,),
            in_specs=[
                pl.BlockSpec((MOE_ROWS, d), lambda b, *_: (b, 0)),
                pl.BlockSpec(memory_space=pl.ANY),
                pl.BlockSpec(memory_space=pl.ANY),
            ],
            out_specs=pl.BlockSpec((MOE_ROWS, d), lambda b, *_: (b, 0)),
            scratch_shapes=[pltpu.VMEM((2, d, 2 * d_expert), F32), pltpu.VMEM((2, d_expert, d), F32),
                            pltpu.VMEM((d, 2 * d_expert), BF16), pltpu.VMEM((d_expert, d), BF16),
                            pltpu.SMEM((1,), I32), pltpu.SemaphoreType.DMA((2, 2))],
        ),
        out_shape=jax.ShapeDtypeStruct((n_blocks * MOE_ROWS, d), F32),
        compiler_params=_params(("arbitrary",)),
        name="moe_experts",
    )(blk_e, n_used, bstart, nblk, buf, w_gate_up, w_down)

    tc = 512
    with_norm = next_gain is not None
    gain = next_gain if with_norm else jnp.ones((1, d), F32)
    row_spec = pl.BlockSpec((tc, d), lambda i: (i, 0))
    out, xn_next = pl.pallas_call(
        functools.partial(_combine_kernel, tc=tc, with_norm=with_norm),
        grid=(tokens // tc,),
        in_specs=[
            pl.BlockSpec((2 * tc,), lambda i: (i,), memory_space=smem),
            pl.BlockSpec((2 * tc,), lambda i: (jnp.minimum(i + 1, tokens // tc - 1),), memory_space=smem),
            pl.BlockSpec(memory_space=pl.ANY),
            pl.BlockSpec((tc, LANES), lambda i: (i, 0)),
            row_spec,
            pl.BlockSpec((1, d), lambda i: (0, 0)),
        ],
        out_specs=[row_spec, row_spec if with_norm else pl.BlockSpec((8, LANES), lambda i: (0, 0))],
        out_shape=[jax.ShapeDtypeStruct((tokens, d), F32),
                   jax.ShapeDtypeStruct((tokens, d), BF16) if with_norm else jax.ShapeDtypeStruct((8, LANES), BF16)],
        scratch_shapes=[pltpu.VMEM((2, 2, tc, d), F32), pltpu.SemaphoreType.DMA((2,))],
        compiler_params=_params(("arbitrary",)),
        name="moe_combine",
    )(dest_flat, dest_flat, y, route, h1, gain)
    return out, (xn_next if with_norm else None)


def _router_weights(w_group, b_group, w_expert, b_expert):
    d = w_group.shape[0]
    pad = LANES - N_EXPERTS - N_GROUPS
    w = jnp.concatenate([w_expert, w_group, jnp.zeros((d, pad), F32)], axis=1)
    b = jnp.concatenate([b_expert, b_group, jnp.zeros((pad,), F32)])[None, :]
    return w, b


def _row(v):
    return v.astype(F32)[None, :]


def kernel(x, positions, l0_norm_mix, l0_mla_w_in, l0_mla_q_lat_norm, l0_mla_kv_lat_norm, l0_mla_w_uq, l0_mla_w_ukv, l0_mla_qk_gain, l0_mla_w_o, l0_norm_ffn, l0_router_group, l0_router_group_bias, l0_router_expert, l0_router_expert_bias, l0_w_gate_up, l0_w_down, l1_norm_mix, l1_fox_w_in, l1_fox_forget_bias, l1_fox_qk_gain, l1_fox_w_o, l1_norm_ffn, l1_router_group, l1_router_group_bias, l1_router_expert, l1_router_expert_bias, l1_w_gate_up, l1_w_down):
    batch, seq, d = x.shape
    tokens = batch * seq
    q_lora = l0_mla_q_lat_norm.shape[0]
    kv_lora = l0_mla_kv_lat_norm.shape[0]
    rope = l0_mla_w_in.shape[1] - q_lora - kv_lora
    nope = l0_mla_qk_gain.shape[1] - rope
    heads = l0_mla_w_uq.shape[1] // (nope + rope)
    assert rope == 64 and nope == LANES and heads % 2 == 0
    tk_attn = min(512, seq // 2)

    h0 = x.reshape(tokens, d)
    pos = positions.reshape(tokens, 1)

    half = rope // 2
    freqs = ROPE_THETA ** (-jnp.arange(half, dtype=F32) / half)
    freq_row = jnp.tile(freqs, LANES // half)[None, :]
    gkr = _row(l0_mla_qk_gain[1, nope:])
    tm = 512
    row = lambda w: pl.BlockSpec((tm, w), lambda i: (i, 0))
    const = lambda shape: pl.BlockSpec(shape, lambda i: (0, 0))
    win = l0_mla_w_in.astype(BF16)
    cq, ckv, kr2, cos_t, sin_t = pl.pallas_call(
        functools.partial(_mla_in_kernel, q_lora=q_lora, kv_lora=kv_lora),
        grid=(tokens // tm,),
        in_specs=[row(d), row(1), const((1, d)), const(win.shape), const((1, q_lora)), const((1, kv_lora)),
                  const((1, rope)), const((1, LANES))],
        out_specs=[row(q_lora), row(kv_lora), row(LANES), row(LANES), row(LANES)],
        out_shape=[jax.ShapeDtypeStruct((tokens, q_lora), BF16), jax.ShapeDtypeStruct((tokens, kv_lora), BF16),
                   jax.ShapeDtypeStruct((tokens, LANES), BF16), jax.ShapeDtypeStruct((tokens, LANES), F32),
                   jax.ShapeDtypeStruct((tokens, LANES), F32)],
        compiler_params=_params(("parallel",)),
        name="mla_in",
    )(h0, pos, _row(l0_norm_mix), win, _row(l0_mla_q_lat_norm), _row(l0_mla_kv_lat_norm), gkr, freq_row)

    wuq = l0_mla_w_uq.reshape(q_lora, heads, nope + rope)
    wuq = jnp.concatenate([wuq[:, :, :nope].reshape(q_lora, heads * nope),
                           wuq[:, :, nope:].reshape(q_lora, heads * rope)], axis=1).astype(BF16)
    scale = LOG2E / math.sqrt(nope + rope)
    q_aug = pl.pallas_call(
        functools.partial(_mla_q_kernel, heads=heads, scale=scale),
        grid=(tokens // tm,),
        in_specs=[row(q_lora), const(wuq.shape), const((1, nope)), const((1, LANES)), row(LANES), row(LANES)],
        out_specs=row(heads * HEAD_PAD),
        out_shape=jax.ShapeDtypeStruct((tokens, heads * HEAD_PAD), BF16),
        compiler_params=_params(("parallel",)),
        name="mla_q",
    )(cq, wuq, _row(l0_mla_qk_gain[0, :nope]), _row(jnp.tile(l0_mla_qk_gain[0, nope:], 2)), cos_t, sin_t)

    v_dim = l0_mla_w_ukv.shape[1] // heads - nope
    assert v_dim == LANES
    wukv = l0_mla_w_ukv.reshape(kv_lora, heads, nope + v_dim)
    wukv = jnp.concatenate([wukv[:, :, :nope].reshape(kv_lora, heads * nope),
                            wukv[:, :, nope:].reshape(kv_lora, heads * v_dim)], axis=1).astype(BF16)
    k_aug, v0 = pl.pallas_call(
        functools.partial(_mla_kv_kernel, heads=heads),
        grid=(tokens // tm,),
        in_specs=[row(kv_lora), const(wukv.shape), const((1, nope)), row(LANES)],
        out_specs=[row(heads * HEAD_PAD), row(heads * v_dim)],
        out_shape=[jax.ShapeDtypeStruct((tokens, heads * HEAD_PAD), BF16),
                   jax.ShapeDtypeStruct((tokens, heads * v_dim), BF16)],
        compiler_params=_params(("parallel",)),
        name="mla_kv",
    )(ckv, wukv, _row(l0_mla_qk_gain[1, :nope]), kr2)

    o0 = _attention(q_aug, k_aug, v0, batch=batch, seq=seq, heads=heads, tk=tk_attn)

    wr0, br0 = _router_weights(l0_router_group, l0_router_group_bias, l0_router_expert, l0_router_expert_bias)
    h1, xn1, route0, cnt0 = _oproj_router(o0, None, h0, l0_mla_w_o.astype(BF16), _row(l0_norm_ffn), wr0, br0, tm=512)
    h2, xn2 = _moe(xn1, route0, cnt0, h1, l0_w_gate_up, l0_w_down, _row(l1_norm_mix))

    fheads = l1_fox_forget_bias.shape[0]
    hd = l1_fox_qk_gain.shape[1]
    assert hd == LANES
    hh = fheads * hd
    w1 = l1_fox_w_in
    wq1, wk1, wv1 = (w1[:, i * hh:(i + 1) * hh].astype(BF16) for i in range(3))
    wf1 = jnp.concatenate([w1[:, 3 * hh:3 * hh + fheads], jnp.zeros((d, LANES - fheads), F32)], axis=1).astype(BF16)
    wg1 = w1[:, 3 * hh + fheads:].astype(BF16)
    fbias = jnp.concatenate([l1_fox_forget_bias.astype(F32), jnp.zeros((LANES - fheads,), F32)])[None, :]

    ts = min(512, seq)
    ns = seq // ts
    cdec = pl.pallas_call(
        functools.partial(_fox_decay_kernel, ts=ts),
        grid=(batch, ns),
        in_specs=[pl.BlockSpec((ts, d), lambda b, i: (b * ns + i, 0)), pl.BlockSpec((d, LANES), lambda b, i: (0, 0)),
                  pl.BlockSpec((1, LANES), lambda b, i: (0, 0))],
        out_specs=pl.BlockSpec((ts, LANES), lambda b, i: (b * ns + i, 0)),
        out_shape=jax.ShapeDtypeStruct((tokens, LANES), F32),
        scratch_shapes=[pltpu.VMEM((1, LANES), F32)],
        compiler_params=_params(("arbitrary", "arbitrary")),
        name="fox_decay",
    )(xn2, wf1, fbias)

    def fox_proj(w, gain, mode, out_width, out_dtype):
        return pl.pallas_call(
            functools.partial(_fox_proj_kernel, heads=fheads, mode=mode,
                              scale=(LOG2E / math.sqrt(hd)) if mode == "q" else 1.0),
            grid=(tokens // tm,),
            in_specs=[row(d), const(w.shape), const((1, hd)), row(LANES)],
            out_specs=row(out_width),
            out_shape=jax.ShapeDtypeStruct((tokens, out_width), out_dtype),
            compiler_params=_params(("parallel",)),
            name="fox_proj_" + mode,
        )(xn2, w, gain, cdec)

    q1 = fox_proj(wq1, _row(l1_fox_qk_gain[0]), "q", fheads * HEAD_PAD, BF16)
    k1 = fox_proj(wk1, _row(l1_fox_qk_gain[1]), "k", fheads * HEAD_PAD, BF16)
    v1 = fox_proj(wv1, _row(l1_fox_qk_gain[0]), "v", hh, BF16)
    gate1 = fox_proj(wg1, _row(l1_fox_qk_gain[0]), "gate", hh, F32)

    o1 = _attention(q1, k1, v1, batch=batch, seq=seq, heads=fheads, tk=tk_attn)

    wr1, br1 = _router_weights(l1_router_group, l1_router_group_bias, l1_router_expert, l1_router_expert_bias)
    h3, xn3, route1, cnt1 = _oproj_router(o1, gate1, h2, l1_fox_w_o.astype(BF16), _row(l1_norm_ffn), wr1, br1, tm=512)
    h4, _ = _moe(xn3, route1, cnt1, h3, l1_w_gate_up, l1_w_down, None)
    return h4.reshape(batch, seq, d)
```

```python
import functools
import math
from typing import NamedTuple

import jax
import jax.numpy as jnp
from jax import lax
from jax.experimental import pallas as pl
from jax.experimental.pallas import tpu as pltpu

F32 = jnp.float32
BF16 = jnp.bfloat16
I32 = jnp.int32

EPS = 1e-6
ROPE_THETA = 10000.0
LANES = 128
SUBLANES = 8
HEAD_PAD = 256
N_GROUPS = 8
EXPERTS_PER_GROUP = 8
N_EXPERTS = N_GROUPS * EXPERTS_PER_GROUP
MOE_ROWS = 256
NEG_BIG = -1e30
LOG2E = math.log2(math.e)
VMEM_LIMIT = 52 * 1024 * 1024


class _Tiles(NamedTuple):
    proj: int
    attn_keys: int
    router: int
    dest: int
    dispatch: int
    combine: int
    decay: int


def _tiles(tokens, seq):
    fit = lambda want: math.gcd(want, tokens)
    return _Tiles(proj=fit(512), attn_keys=math.gcd(512, seq // 2), router=fit(512), dest=fit(512),
                  dispatch=fit(1024), combine=fit(512), decay=math.gcd(512, seq))


def _params(sem, vmem=VMEM_LIMIT):
    return pltpu.CompilerParams(dimension_semantics=sem, vmem_limit_bytes=vmem)


def _rms(x, gain):
    ms = jnp.mean(x * x, axis=-1, keepdims=True)
    return x * lax.rsqrt(ms + EPS) * gain


def _lane_iota(shape):
    return lax.broadcasted_iota(I32, shape, len(shape) - 1)


def _rope_slab(x, cos, sin_signed):
    lane = _lane_iota((1, LANES))
    low = (lane % 64) < 32
    swapped = jnp.where(low, pltpu.roll(x, 96, 1), pltpu.roll(x, 32, 1))
    return x * cos + swapped * sin_signed


def _split3(x):
    a = x.astype(BF16)
    r = x - a.astype(F32)
    b = r.astype(BF16)
    c = (r - b.astype(F32)).astype(BF16)
    return a, b, c


def _mla_in_kernel(h_ref, pos_ref, gmix_ref, win_ref, gq_ref, gkv_ref, gkr_ref, freq_ref,
                   cq_ref, ckv_ref, kr_ref, cos_ref, sin_ref, *, q_lora, kv_lora):
    xn = _rms(h_ref[...], gmix_ref[...]).astype(BF16)
    z = jnp.dot(xn, win_ref[...], preferred_element_type=F32)
    cq_ref[...] = _rms(z[:, :q_lora], gq_ref[...]).astype(BF16)
    ckv_ref[...] = _rms(z[:, q_lora:q_lora + kv_lora], gkv_ref[...]).astype(BF16)
    kr = _rms(z[:, q_lora + kv_lora:], gkr_ref[...])
    slab = jnp.concatenate([kr, kr], axis=-1)
    ang = pos_ref[...].astype(F32) * freq_ref[...]
    lane = _lane_iota((1, LANES))
    cos = jnp.cos(ang)
    sin_signed = jnp.where((lane % 64) < 32, -jnp.sin(ang), jnp.sin(ang))
    cos_ref[...] = cos
    sin_ref[...] = sin_signed
    kr_ref[...] = _rope_slab(slab, cos, sin_signed).astype(BF16)


def _mla_q_kernel(cq_ref, w_ref, gn_ref, gr_ref, cos_ref, sin_ref, q_ref, *, heads, scale):
    q = jnp.dot(cq_ref[...], w_ref[...], preferred_element_type=F32)
    lane = _lane_iota((1, LANES))
    low = lane < 64
    cos = cos_ref[...]
    sin_signed = sin_ref[...]
    for h in range(heads):
        n = _rms(q[:, h * LANES:(h + 1) * LANES], gn_ref[...]) * scale
        q_ref[:, h * HEAD_PAD:h * HEAD_PAD + LANES] = n.astype(BF16)
    base = heads * LANES
    for p in range(heads // 2):
        slab = q[:, base + p * LANES:base + (p + 1) * LANES]
        sq = slab * slab
        ss0 = jnp.sum(jnp.where(low, sq, 0.0), axis=-1, keepdims=True)
        ss1 = jnp.sum(jnp.where(low, 0.0, sq), axis=-1, keepdims=True)
        inv = jnp.where(low, lax.rsqrt(ss0 / 64.0 + EPS), lax.rsqrt(ss1 / 64.0 + EPS))
        r = _rope_slab(slab * inv * gr_ref[...], cos, sin_signed) * scale
        zero = jnp.zeros_like(r)
        q_ref[:, (2 * p) * HEAD_PAD + LANES:(2 * p + 1) * HEAD_PAD] = jnp.where(low, r, zero).astype(BF16)
        q_ref[:, (2 * p + 1) * HEAD_PAD + LANES:(2 * p + 2) * HEAD_PAD] = jnp.where(low, zero, r).astype(BF16)


def _mla_kv_kernel(ckv_ref, w_ref, gn_ref, kr_ref, k_ref, v_ref, *, heads):
    kv = jnp.dot(ckv_ref[...], w_ref[...], preferred_element_type=F32)
    lane = _lane_iota((1, LANES))
    low = lane < 64
    kr = kr_ref[...]
    zero = jnp.zeros_like(kr)
    kr_even = jnp.where(low, kr, zero)
    kr_odd = jnp.where(low, zero, kr)
    for h in range(heads):
        n = _rms(kv[:, h * LANES:(h + 1) * LANES], gn_ref[...])
        k_ref[:, h * HEAD_PAD:h * HEAD_PAD + LANES] = n.astype(BF16)
        k_ref[:, h * HEAD_PAD + LANES:(h + 1) * HEAD_PAD] = kr_even if h % 2 == 0 else kr_odd
    v_ref[...] = kv[:, heads * LANES:].astype(BF16)


def _fox_decay_kernel(xn_ref, w_ref, b_ref, c_ref, carry_ref, *, ts):
    @pl.when(pl.program_id(1) == 0)
    def _():
        carry_ref[...] = jnp.zeros_like(carry_ref)

    f = jnp.dot(xn_ref[...], w_ref[...], preferred_element_type=F32) + b_ref[...]
    log_f = jnp.minimum(f, 0.0) - jnp.log1p(jnp.exp(-jnp.abs(f)))
    row = lax.broadcasted_iota(I32, (ts, ts), 0)
    col = lax.broadcasted_iota(I32, (ts, ts), 1)
    tri = jnp.where(col <= row, 1.0, 0.0).astype(BF16)
    a, b, c = _split3(log_f)
    parts = jnp.dot(tri, jnp.concatenate([a, b, c], axis=-1), preferred_element_type=F32)
    local = parts[:, :LANES] + parts[:, LANES:2 * LANES] + parts[:, 2 * LANES:]
    out = carry_ref[...] + local
    c_ref[...] = out
    carry_ref[...] = out[ts - 1:ts, :]


def _fox_proj_kernel(xn_ref, w_ref, g_ref, c_ref, o_ref, *, heads, mode, scale):
    z = jnp.dot(xn_ref[...], w_ref[...], preferred_element_type=F32)
    if mode == "v":
        o_ref[...] = z.astype(BF16)
        return
    if mode == "gate":
        o_ref[...] = z
        return
    lane = _lane_iota((1, LANES))
    cdec = c_ref[...] * LOG2E
    for h in range(heads):
        n = _rms(z[:, h * LANES:(h + 1) * LANES], g_ref[...]) * scale
        o_ref[:, h * HEAD_PAD:h * HEAD_PAD + LANES] = n.astype(BF16)
        ch = cdec[:, h:h + 1]
        if mode == "k":
            ch = -ch
        a, b, c = (t.astype(F32) for t in _split3(ch))
        if mode == "q":
            aug = jnp.where(lane == 0, a, jnp.where(lane == 1, b, jnp.where(lane == 2, c,
                  jnp.where(lane < 6, 1.0, 0.0))))
        else:
            aug = jnp.where(lane < 3, 1.0, jnp.where(lane == 3, a, jnp.where(lane == 4, b,
                  jnp.where(lane == 5, c, 0.0))))
        o_ref[:, h * HEAD_PAD + LANES:(h + 1) * HEAD_PAD] = aug.astype(BF16)


def _attn_tile(q, k_ref, v_ref, j, m_ref, l_ref, acc_ref, *, tk, masked):
    rows = q.shape[0]
    start = pl.multiple_of(j * tk, tk)
    k = k_ref[pl.ds(start, tk), :]
    v = v_ref[pl.ds(start, tk), :]
    s = lax.dot_general(q, k, (((1,), (1,)), ((), ())), preferred_element_type=F32)
    if masked:
        row = lax.broadcasted_iota(I32, (rows, tk), 0)
        col = lax.broadcasted_iota(I32, (rows, tk), 1)
        s = jnp.where(row >= col, s, NEG_BIG)
    m_prev = m_ref[...]
    m_new = jnp.maximum(m_prev, jnp.max(s, axis=-1, keepdims=True))
    alpha = jnp.exp2(m_prev - m_new)
    p = jnp.exp2(s - jnp.concatenate([m_new] * (tk // LANES), axis=1))
    psum = p[:, :LANES]
    for c in range(1, tk // LANES):
        psum = psum + p[:, c * LANES:(c + 1) * LANES]
    l_ref[...] = alpha * l_ref[...] + psum
    acc_ref[...] = alpha * acc_ref[...] + jnp.dot(p.astype(BF16), v, preferred_element_type=F32)
    m_ref[...] = m_new


def _attn_kernel(q_ref, k_ref, v_ref, o_ref, m_a, l_a, acc_a, m_b, l_b, acc_b, *, tk):
    qi = pl.program_id(2)
    m_a[...] = jnp.full_like(m_a, NEG_BIG)
    m_b[...] = jnp.full_like(m_b, NEG_BIG)

    @pl.when((pl.program_id(0) == 0) & (pl.program_id(1) == 0) & (qi == 0))
    def _():
        for ref in (l_a, acc_a, l_b, acc_b):
            ref[...] = jnp.zeros_like(ref)

    q = q_ref[...]
    tile = functools.partial(_attn_tile, k_ref=k_ref, v_ref=v_ref, tk=tk)

    def pair(jj):
        tile(q, j=2 * jj, m_ref=m_a, l_ref=l_a, acc_ref=acc_a, masked=False)
        tile(q, j=2 * jj + 1, m_ref=m_b, l_ref=l_b, acc_ref=acc_b, masked=False)

    def body(u, carry):
        pair(2 * u)
        pair(2 * u + 1)
        return carry

    lax.fori_loop(0, qi // 2, body, 0)

    @pl.when(qi % 2 == 1)
    def _():
        pair(qi - 1)

    tile(q, j=2 * qi, m_ref=m_a, l_ref=l_a, acc_ref=acc_a, masked=True)
    low = pl.ds(tk, tk)
    tile(q_ref[low, :], j=2 * qi + 1, m_ref=m_b.at[low, :], l_ref=l_b.at[low, :], acc_ref=acc_b.at[low, :],
         masked=True)
    m = jnp.maximum(m_a[...], m_b[...])
    w_a = jnp.exp2(m_a[...] - m)
    w_b = jnp.exp2(m_b[...] - m)
    l = jnp.sum(w_a * l_a[...] + w_b * l_b[...], axis=-1, keepdims=True)
    o_ref[...] = ((w_a * acc_a[...] + w_b * acc_b[...]) / l).astype(BF16)


def _attention(q, k, v, *, batch, seq, heads, tk):
    tq = 2 * tk
    nq = seq // tq
    stat = pltpu.VMEM((tq, LANES), F32)
    return pl.pallas_call(
        functools.partial(_attn_kernel, tk=tk),
        grid=(batch, heads, nq),
        in_specs=[
            pl.BlockSpec((tq, HEAD_PAD), lambda b, h, i: (b * nq + i, h)),
            pl.BlockSpec((seq, HEAD_PAD), lambda b, h, i: (b, h)),
            pl.BlockSpec((seq, LANES), lambda b, h, i: (b, h)),
        ],
        out_specs=pl.BlockSpec((tq, LANES), lambda b, h, i: (b * nq + i, h)),
        out_shape=jax.ShapeDtypeStruct((batch * seq, heads * LANES), BF16),
        scratch_shapes=[stat] * 6,
        compiler_params=_params(("arbitrary", "arbitrary", "arbitrary")),
        name="causal_attention",
    )(q, k, v)


def _oproj_router_kernel(*refs, tm, gated):
    if gated:
        (o_ref, gate_ref, h_ref, wo_ref, gffn_ref, wr_ref, br_ref,
         h1_ref, xn_ref, route_ref, cnt_ref, carry_ref, wsplit_ref) = refs
    else:
        (o_ref, h_ref, wo_ref, gffn_ref, wr_ref, br_ref,
         h1_ref, xn_ref, route_ref, cnt_ref, carry_ref, wsplit_ref) = refs

    @pl.when(pl.program_id(0) == 0)
    def _():
        carry_ref[...] = jnp.zeros_like(carry_ref)
        w = wr_ref[...]
        w_hi = w.astype(BF16)
        wsplit_ref[:, :LANES] = w_hi
        wsplit_ref[:, LANES:] = (w - w_hi.astype(F32)).astype(BF16)

    o = o_ref[...]
    if gated:
        o = (o.astype(F32) * jax.nn.sigmoid(gate_ref[...])).astype(BF16)
    h1 = h_ref[...] + jnp.dot(o, wo_ref[...], preferred_element_type=F32)
    h1_ref[...] = h1
    xn = _rms(h1, gffn_ref[...])
    xn_ref[...] = xn

    x_hi = xn.astype(BF16)
    x_lo = (xn - x_hi.astype(F32)).astype(BF16)
    part = jnp.dot(x_hi, wsplit_ref[...], preferred_element_type=F32)
    logits = (part[:, :LANES] + part[:, LANES:]
              + jnp.dot(x_lo, wsplit_ref[:, :LANES], preferred_element_type=F32)) + br_ref[...]

    lane_i = _lane_iota((tm, LANES))
    lane = lane_i.astype(F32)
    big = 1e6
    is_g = (lane_i >= N_EXPERTS) & (lane_i < N_EXPERTS + N_GROUPS)
    gl = jnp.where(is_g, logits, NEG_BIG)
    gmax = jnp.max(gl, axis=-1, keepdims=True)
    grp = jnp.min(jnp.where(gl == gmax, lane, big), axis=-1, keepdims=True) - N_EXPERTS
    p_g = 1.0 / jnp.sum(jnp.where(is_g, jnp.exp(gl - gmax), 0.0), axis=-1, keepdims=True)

    in_grp = (lane_i // EXPERTS_PER_GROUP).astype(F32) == grp
    el = jnp.where(in_grp, logits, NEG_BIG)
    emax = jnp.max(el, axis=-1, keepdims=True)
    pe = jnp.where(in_grp, jnp.exp(el - emax), 0.0)
    pe = pe / jnp.sum(pe, axis=-1, keepdims=True)
    pm = jnp.where(in_grp, pe, -1.0)
    p1 = jnp.max(pm, axis=-1, keepdims=True)
    i1 = jnp.min(jnp.where(pm == p1, lane, big), axis=-1, keepdims=True)
    pm2 = jnp.where(lane == i1, -1.0, pm)
    p2 = jnp.max(pm2, axis=-1, keepdims=True)
    i2 = jnp.min(jnp.where(pm2 == p2, lane, big), axis=-1, keepdims=True)
    den = p1 + p2
    g1 = p_g * p1 / den
    g2 = p_g * p2 / den

    hit1 = lane == i1
    hit2 = lane == i2
    oh1 = jnp.where(hit1, 1.0, 0.0)
    oh2 = jnp.where(hit2, 1.0, 0.0)
    row = lax.broadcasted_iota(I32, (tm, tm), 0)
    col = lax.broadcasted_iota(I32, (tm, tm), 1)
    tri = jnp.where(col < row, 1.0, 0.0).astype(BF16)
    cum1 = jnp.dot(tri, oh1.astype(BF16), preferred_element_type=F32)
    cum2 = jnp.dot(tri, oh2.astype(BF16), preferred_element_type=F32)
    tot1 = jnp.sum(oh1, axis=0, keepdims=True)
    tot2 = jnp.sum(oh2, axis=0, keepdims=True)
    carry = carry_ref[...]
    r1 = jnp.sum(jnp.where(hit1, carry + cum1, 0.0), axis=-1, keepdims=True)
    r2 = jnp.sum(jnp.where(hit2, carry + tot1 + cum2, 0.0), axis=-1, keepdims=True)
    new_carry = carry + tot1 + tot2
    carry_ref[...] = new_carry
    cnt_ref[...] = new_carry

    zero = jnp.zeros((tm, LANES), F32)
    route = jnp.where(lane_i == 0, i1, jnp.where(lane_i == 1, i2,
            jnp.where(lane_i == 2, r1, jnp.where(lane_i == 3, r2,
            jnp.where(lane_i == 4, g1, jnp.where(lane_i == 5, g2, zero))))))
    route_ref[...] = route


def _oproj_router(o, gate, h, wo, gffn, wr, br, *, tm):
    tokens, d = h.shape
    gated = gate is not None
    row_spec = pl.BlockSpec((tm, d), lambda i: (i, 0))
    const = lambda shape: pl.BlockSpec(shape, lambda i: (0, 0))
    in_specs = [row_spec] + ([row_spec] if gated else []) + [
        row_spec, const(wo.shape), const((1, d)), const(wr.shape), const((1, LANES))]
    args = [o] + ([gate] if gated else []) + [h, wo, gffn, wr, br]
    return pl.pallas_call(
        functools.partial(_oproj_router_kernel, tm=tm, gated=gated),
        grid=(tokens // tm,),
        in_specs=in_specs,
        out_specs=[row_spec, row_spec, pl.BlockSpec((tm, LANES), lambda i: (i, 0)), const((1, LANES))],
        out_shape=[jax.ShapeDtypeStruct((tokens, d), F32), jax.ShapeDtypeStruct((tokens, d), F32),
                   jax.ShapeDtypeStruct((tokens, LANES), F32), jax.ShapeDtypeStruct((1, LANES), F32)],
        scratch_shapes=[pltpu.VMEM((1, LANES), F32), pltpu.VMEM((d, 2 * LANES), BF16)],
        compiler_params=_params(("arbitrary",)),
        name="oproj_router",
    )(*args)


def _plan_kernel(cnt_ref, plan_ref, blk_ref, *, n_blocks):
    cnt = cnt_ref[...]
    nblk = jnp.floor((cnt + (MOE_ROWS - 1)) * (1.0 / MOE_ROWS))
    row = lax.broadcasted_iota(I32, (LANES, LANES), 0)
    col = lax.broadcasted_iota(I32, (LANES, LANES), 1)
    upper = jnp.where(row < col, 1.0, 0.0).astype(BF16)
    nb8 = jnp.broadcast_to(nblk, (8, LANES)).astype(BF16)
    bstart = jnp.dot(nb8, upper, preferred_element_type=F32)[0:1, :]
    bend = bstart + nblk
    lane = _lane_iota((1, LANES))
    total = jnp.sum(nblk, axis=-1, keepdims=True)
    sub = lax.broadcasted_iota(I32, (8, LANES), 0)
    plan = jnp.where(sub == 0, jnp.broadcast_to(bstart, (8, LANES)),
           jnp.where(sub == 1, jnp.broadcast_to(nblk, (8, LANES)),
           jnp.where(sub == 2, jnp.broadcast_to(total, (8, LANES)), 0.0)))
    plan_ref[...] = plan
    bidx = lax.broadcasted_iota(I32, (n_blocks, LANES), 0).astype(F32)
    done = jnp.where((_lane_iota((n_blocks, LANES)) < N_EXPERTS) & (bend <= bidx), 1.0, 0.0)
    last_used = jnp.max(jnp.where(nblk > 0.0, lane.astype(F32), 0.0), axis=-1, keepdims=True)
    e_of_b = jnp.minimum(jnp.sum(done, axis=-1, keepdims=True), last_used)
    blk_ref[...] = jnp.broadcast_to(e_of_b, (n_blocks, LANES))


def _dest_kernel(route_ref, plan_ref, dest_ref, *, tm):
    route = route_ref[...]
    bstart = plan_ref[0:1, :] * float(MOE_ROWS)
    lane = _lane_iota((tm, LANES)).astype(F32)
    e1 = route[:, 0:1]
    e2 = route[:, 1:2]
    d1 =jnp.sum(jnp.where(lane == e1, bstart, 0.0), axis=-1, keepdims=True) + route[:, 2:3]
    d2 = jnp.sum(jnp.where(lane == e2, bstart, 0.0), axis=-1, keepdims=True) + route[:, 3:4]
    dest_ref[...] = jnp.where(lane == 0, d1, jnp.where(lane == 1, d2, 0.0))


def _dispatch_kernel(dest_ref, bstart_ref, nblk_ref, cnt_ref, nused_ref, x_ref, buf_hbm, zeros_ref, sem_zero,
                     sem_rows, *, td, n_blocks):
    step = pl.program_id(0)

    def for_each_fill(fn):
        def per_expert(e, c):
            base = bstart_ref[e] * MOE_ROWS + cnt_ref[e]
            head = (8 - (base & 7)) & 7
            for r in range(7):
                @pl.when(r < head)
                def _(r=r):
                    fn(pltpu.make_async_copy(zeros_ref.at[pl.ds(0, 1), :], buf_hbm.at[pl.ds(base + r, 1), :],
                                             sem_zero))
            length = (bstart_ref[e] + nblk_ref[e]) * MOE_ROWS - (base + head)
            off = base + head
            for p in (128, 64, 32, 16, 8):
                @pl.when((length & p) != 0)
                def _(off=off, p=p):
                    fn(pltpu.make_async_copy(zeros_ref.at[pl.ds(0, p), :],
                                             buf_hbm.at[pl.ds(pl.multiple_of(off, 8), p), :], sem_zero))
                off = off + (length & p)
            return c

        lax.fori_loop(0, N_EXPERTS, per_expert, 0)

        def per_tail_block(blk, c):
            fn(pltpu.make_async_copy(
                zeros_ref, buf_hbm.at[pl.ds(pl.multiple_of(blk * MOE_ROWS, MOE_ROWS), MOE_ROWS), :], sem_zero))
            return c

        lax.fori_loop(nused_ref[0], n_blocks, per_tail_block, 0)

    @pl.when(step == 0)
    def _():
        zeros_ref[...] = jnp.zeros_like(zeros_ref)
        for_each_fill(lambda copy: copy.start())

    def start_group(i, c):
        for r in range(SUBLANES):
            for k in range(2):
                pltpu.make_async_copy(x_ref.at[i, pl.ds(r, 1), :],
                                      buf_hbm.at[pl.ds(dest_ref[2 * (SUBLANES * i + r) + k], 1), :], sem_rows).start()
        return c

    lax.fori_loop(0, td // SUBLANES, start_group, 0)
    for _ in range(2):
        pltpu.make_async_copy(buf_hbm.at[pl.ds(0, td), :], buf_hbm.at[pl.ds(0, td), :], sem_rows).wait()

    @pl.when(step == pl.num_programs(0) - 1)
    def _():
        for_each_fill(lambda copy: copy.wait())


def _expert_kernel(blk_ref, nused_ref, bstart_ref, nblk_ref, x_ref, wgu_hbm, wdn_hbm, y_ref,
                   wgu_f32, wdn_f32, wgu_bf, wdn_bf, slot_ref, sems, *, d_expert):
    b = pl.program_id(0)

    def weight_copies(e, slot):
        return (pltpu.make_async_copy(wgu_hbm.at[e], wgu_f32.at[slot], sems.at[0, slot]),
                pltpu.make_async_copy(wdn_hbm.at[e], wdn_f32.at[slot], sems.at[1, slot]))

    @pl.when(b == 0)
    def _():
        slot_ref[0] = 0
        for c in weight_copies(blk_ref[0], 0):
            c.start(priority=1)

    @pl.when(b < nused_ref[0])
    def _():
        e = blk_ref[b]
        first = (b == 0) | (e != blk_ref[jnp.maximum(b - 1, 0)])

        @pl.when(first)
        def _():
            slot = slot_ref[0]
            nxt = bstart_ref[e] + nblk_ref[e]

            @pl.when(nxt < nused_ref[0])
            def _():
                for c in weight_copies(blk_ref[nxt], 1 - slot):
                    c.start(priority=1)

            for c in weight_copies(e, slot):
                c.wait()
            wgu_bf[...] = wgu_f32[slot].astype(BF16)
            wdn_bf[...] = wdn_f32[slot].astype(BF16)
            slot_ref[0] = 1 - slot

        x = x_ref[...].astype(BF16)
        gu = jnp.dot(x, wgu_bf[...], preferred_element_type=F32)
        g = gu[:, :d_expert]
        act = (g * jax.nn.sigmoid(g)) * gu[:, d_expert:]
        y_ref[...] = jnp.dot(act.astype(BF16), wdn_bf[...], preferred_element_type=F32)

    @pl.when(b >= nused_ref[0])
    def _():
        y_ref[...] = jnp.zeros_like(y_ref)


def _combine_kernel(dest_ref, dest_next_ref, y_hbm, route_ref, h_ref, g_ref, out_ref, xn_ref, rows_ref, sems,
                    *, tc, with_norm):
    step = pl.program_id(0)
    nsteps = pl.num_programs(0)
    slot = step % 2

    def issue(idx_ref, to_slot):
        def start_group(i, c):
            for r in range(SUBLANES):
                for k in range(2):
                    pltpu.make_async_copy(y_hbm.at[pl.ds(idx_ref[2 * (SUBLANES * i + r) + k], 1), :],
                                          rows_ref.at[to_slot, k, i, pl.ds(r, 1), :], sems.at[to_slot]).start()
            return c
        lax.fori_loop(0, tc // SUBLANES, start_group, 0)

    @pl.when(step == 0)
    def _():
        issue(dest_ref, 0)

    for to_slot in range(2):
        @pl.when((step + 1 < nsteps) & (slot == 1 - to_slot))
        def _(to_slot=to_slot):
            issue(dest_next_ref, to_slot)

    for k in range(2):
        pltpu.make_async_copy(y_hbm.at[pl.ds(0, tc), :], y_hbm.at[pl.ds(0, tc), :], sems.at[slot]).wait()
    route = route_ref[...]
    d = h_ref.shape[1]
    y0 = rows_ref[slot, 0].reshape(tc, d)
    y1 = rows_ref[slot, 1].reshape(tc, d)
    out = h_ref[...] + (y0 * route[:, 4:5] + y1 * route[:, 5:6])
    out_ref[...] = out
    if with_norm:
        xn_ref[...] = _rms(out, g_ref[...]).astype(BF16)
    else:
        xn_ref[...] = jnp.zeros_like(xn_ref)


def _moe(xn, route, cnt, h1, w_gate_up, w_down, next_gain, tiles):
    tokens, d = xn.shape
    d_expert = w_down.shape[1]
    n_blocks = (2 * tokens + N_EXPERTS * (MOE_ROWS - 1) + MOE_ROWS - 1) // MOE_ROWS
    assert n_blocks <= 256, "block counts must stay exact in bf16"
    nb_pad = (n_blocks + 7) // 8 * 8

    plan, blk = pl.pallas_call(
        functools.partial(_plan_kernel, n_blocks=nb_pad),
        out_shape=[jax.ShapeDtypeStruct((8, LANES), F32), jax.ShapeDtypeStruct((nb_pad, LANES), F32)],
        name="moe_plan",
    )(cnt)

    tm = tiles.dest
    dest = pl.pallas_call(
        functools.partial(_dest_kernel, tm=tm),
        grid=(tokens // tm,),
        in_specs=[pl.BlockSpec((tm, LANES), lambda i: (i, 0)), pl.BlockSpec((8, LANES), lambda i: (0, 0))],
        out_specs=pl.BlockSpec((tm, LANES), lambda i: (i, 0)),
        out_shape=jax.ShapeDtypeStruct((tokens, LANES), F32),
        compiler_params=_params(("parallel",)),
        name="moe_dest",
    )(route, plan)

    dest_flat = dest[:, :2].astype(I32).reshape(-1)
    bstart = plan[0, :N_EXPERTS].astype(I32)
    nblk = plan[1, :N_EXPERTS].astype(I32)
    n_used = plan[2, :1].astype(I32)
    blk_e = blk[:n_blocks, 0].astype(I32)

    td = tiles.dispatch
    smem = pltpu.SMEM
    buf = pl.pallas_call(
        functools.partial(_dispatch_kernel, td=td, n_blocks=n_blocks),
        grid=(tokens // td,),
        in_specs=[
            pl.BlockSpec((2 * td,), lambda i: (i,), memory_space=smem),
            pl.BlockSpec((N_EXPERTS,), lambda i: (0,), memory_space=smem),
            pl.BlockSpec((N_EXPERTS,), lambda i: (0,), memory_space=smem),
            pl.BlockSpec((N_EXPERTS,), lambda i: (0,), memory_space=smem),
            pl.BlockSpec((1,), lambda i: (0,), memory_space=smem),
            pl.BlockSpec((td // SUBLANES, SUBLANES, d), lambda i: (i, 0, 0)),
        ],
        out_specs=pl.BlockSpec(memory_space=pl.ANY),
        out_shape=jax.ShapeDtypeStruct((n_blocks * MOE_ROWS, d), F32),
        scratch_shapes=[pltpu.VMEM((MOE_ROWS, d), F32), pltpu.SemaphoreType.DMA(()), pltpu.SemaphoreType.DMA(())],
        compiler_params=_params(("arbitrary",)),
        name="moe_dispatch",
    )(dest_flat, bstart, nblk, cnt[0, :N_EXPERTS].astype(I32), n_used, xn.reshape(tokens // SUBLANES, SUBLANES, d))

    y = pl.pallas_call(
        functools.partial(_expert_kernel, d_expert=d_expert),
        grid_spec=pltpu.PrefetchScalarGridSpec(
            num_scalar_prefetch=4,
            grid=(n_blocks,),
            in_specs=[
                pl.BlockSpec((MOE_ROWS, d), lambda b, *_: (b, 0)),
                pl.BlockSpec(memory_space=pl.ANY),
                pl.BlockSpec(memory_space=pl.ANY),
            ],
            out_specs=pl.BlockSpec((MOE_ROWS, d), lambda b, *_: (b, 0)),
            scratch_shapes=[pltpu.VMEM((2, d, 2 * d_expert), F32), pltpu.VMEM((2, d_expert, d), F32),
                            pltpu.VMEM((d, 2 * d_expert), BF16), pltpu.VMEM((d_expert, d), BF16),
                            pltpu.SMEM((1,), I32), pltpu.SemaphoreType.DMA((2, 2))],
        ),
        out_shape=jax.ShapeDtypeStruct((n_blocks * MOE_ROWS, d), F32),
        compiler_params=_params(("arbitrary",)),
        name="moe_experts",
    )(blk_e, n_used, bstart, nblk, buf, w_gate_up, w_down)

    tc = tiles.combine
    with_norm = next_gain is not None
    gain = next_gain if with_norm else jnp.ones((1, d), F32)
    row_spec = pl.BlockSpec((tc, d), lambda i: (i, 0))
    out, xn_next = pl.pallas_call(
        functools.partial(_combine_kernel, tc=tc, with_norm=with_norm),
        grid=(tokens // tc,),
        in_specs=[
            pl.BlockSpec((2 * tc,), lambda i: (i,), memory_space=smem),
            pl.BlockSpec((2 * tc,), lambda i: (jnp.minimum(i + 1, tokens // tc - 1),), memory_space=smem),
            pl.BlockSpec(memory_space=pl.ANY),
            pl.BlockSpec((tc, LANES), lambda i: (i, 0)),
            row_spec,
            pl.BlockSpec((1, d), lambda i: (0, 0)),
        ],
        out_specs=[row_spec, row_spec if with_norm else pl.BlockSpec((8, LANES), lambda i: (0, 0))],
        out_shape=[jax.ShapeDtypeStruct((tokens, d), F32),
                   jax.ShapeDtypeStruct((tokens, d), BF16) if with_norm else jax.ShapeDtypeStruct((8, LANES), BF16)],
        scratch_shapes=[pltpu.VMEM((2, 2, tc // SUBLANES, SUBLANES, d), F32), pltpu.SemaphoreType.DMA((2,))],
        compiler_params=_params(("arbitrary",)),
        name="moe_combine",
    )(dest_flat, dest_flat, y, route, h1, gain)
    return out, (xn_next if with_norm else None)


def _router_weights(w_group, b_group, w_expert, b_expert):
    d = w_group.shape[0]
    pad = LANES - N_EXPERTS - N_GROUPS
    w = jnp.concatenate([w_expert, w_group, jnp.zeros((d, pad), F32)], axis=1)
    b = jnp.concatenate([b_expert, b_group, jnp.zeros((pad,), F32)])[None, :]
    return w, b


def _row(v):
    return v.astype(F32)[None, :]


def kernel(x, positions, l0_norm_mix, l0_mla_w_in, l0_mla_q_lat_norm, l0_mla_kv_lat_norm, l0_mla_w_uq, l0_mla_w_ukv, l0_mla_qk_gain, l0_mla_w_o, l0_norm_ffn, l0_router_group, l0_router_group_bias, l0_router_expert, l0_router_expert_bias, l0_w_gate_up, l0_w_down, l1_norm_mix, l1_fox_w_in, l1_fox_forget_bias, l1_fox_qk_gain, l1_fox_w_o, l1_norm_ffn, l1_router_group, l1_router_group_bias, l1_router_expert, l1_router_expert_bias, l1_w_gate_up, l1_w_down):
    batch, seq, d = x.shape
    tokens = batch * seq
    q_lora = l0_mla_q_lat_norm.shape[0]
    kv_lora = l0_mla_kv_lat_norm.shape[0]
    rope = l0_mla_w_in.shape[1] - q_lora - kv_lora
    nope = l0_mla_qk_gain.shape[1] - rope
    heads = l0_mla_w_uq.shape[1] // (nope + rope)
    assert rope == 64 and nope == LANES and heads % 2 == 0
    tiles = _tiles(tokens, seq)

    h0 = x.reshape(tokens, d)
    pos = positions.reshape(tokens, 1)

    half = rope // 2
    freqs = ROPE_THETA ** (-jnp.arange(half, dtype=F32) / half)
    freq_row = jnp.tile(freqs, LANES // half)[None, :]
    gkr = _row(l0_mla_qk_gain[1, nope:])
    tm = tiles.proj
    row = lambda w: pl.BlockSpec((tm, w), lambda i: (i, 0))
    const = lambda shape: pl.BlockSpec(shape, lambda i: (0, 0))
    win = l0_mla_w_in.astype(BF16)
    cq, ckv, kr2, cos_t, sin_t = pl.pallas_call(
        functools.partial(_mla_in_kernel, q_lora=q_lora, kv_lora=kv_lora),
        grid=(tokens // tm,),
        in_specs=[row(d), row(1), const((1, d)), const(win.shape), const((1, q_lora)), const((1, kv_lora)),
                  const((1, rope)), const((1, LANES))],
        out_specs=[row(q_lora), row(kv_lora), row(LANES), row(LANES), row(LANES)],
        out_shape=[jax.ShapeDtypeStruct((tokens, q_lora), BF16), jax.ShapeDtypeStruct((tokens, kv_lora), BF16),
                   jax.ShapeDtypeStruct((tokens, LANES), BF16), jax.ShapeDtypeStruct((tokens, LANES), F32),
                   jax.ShapeDtypeStruct((tokens, LANES), F32)],
        compiler_params=_params(("parallel",)),
        name="mla_in",
    )(h0, pos, _row(l0_norm_mix), win, _row(l0_mla_q_lat_norm), _row(l0_mla_kv_lat_norm), gkr, freq_row)

    wuq = l0_mla_w_uq.reshape(q_lora, heads, nope + rope)
    wuq = jnp.concatenate([wuq[:, :, :nope].reshape(q_lora, heads * nope),
                           wuq[:, :, nope:].reshape(q_lora, heads * rope)], axis=1).astype(BF16)
    scale = LOG2E / math.sqrt(nope + rope)
    q_aug = pl.pallas_call(
        functools.partial(_mla_q_kernel, heads=heads, scale=scale),
        grid=(tokens // tm,),
        in_specs=[row(q_lora), const(wuq.shape), const((1, nope)), const((1, LANES)), row(LANES), row(LANES)],
        out_specs=row(heads * HEAD_PAD),
        out_shape=jax.ShapeDtypeStruct((tokens, heads * HEAD_PAD), BF16),
        compiler_params=_params(("parallel",)),
        name="mla_q",
    )(cq, wuq, _row(l0_mla_qk_gain[0, :nope]), _row(jnp.tile(l0_mla_qk_gain[0, nope:], 2)), cos_t, sin_t)

    v_dim = l0_mla_w_ukv.shape[1] // heads - nope
    assert v_dim == LANES
    wukv = l0_mla_w_ukv.reshape(kv_lora, heads, nope + v_dim)
    wukv = jnp.concatenate([wukv[:, :, :nope].reshape(kv_lora, heads * nope),
                            wukv[:, :, nope:].reshape(kv_lora, heads * v_dim)], axis=1).astype(BF16)
    k_aug, v0 = pl.pallas_call(
        functools.partial(_mla_kv_kernel, heads=heads),
        grid=(tokens // tm,),
        in_specs=[row(kv_lora), const(wukv.shape), const((1, nope)), row(LANES)],
        out_specs=[row(heads * HEAD_PAD), row(heads * v_dim)],
        out_shape=[jax.ShapeDtypeStruct((tokens, heads * HEAD_PAD), BF16),
                   jax.ShapeDtypeStruct((tokens, heads * v_dim), BF16)],
        compiler_params=_params(("parallel",)),
        name="mla_kv",
    )(ckv, wukv, _row(l0_mla_qk_gain[1, :nope]), kr2)

    o0 = _attention(q_aug, k_aug, v0, batch=batch, seq=seq, heads=heads, tk=tiles.attn_keys)

    wr0, br0 = _router_weights(l0_router_group, l0_router_group_bias, l0_router_expert, l0_router_expert_bias)
    h1, xn1, route0, cnt0 = _oproj_router(o0, None, h0, l0_mla_w_o.astype(BF16), _row(l0_norm_ffn), wr0, br0, tm=tiles.router)
    h2, xn2 = _moe(xn1, route0, cnt0, h1, l0_w_gate_up, l0_w_down, _row(l1_norm_mix), tiles)

    fheads = l1_fox_forget_bias.shape[0]
    hd = l1_fox_qk_gain.shape[1]
    assert hd == LANES
    hh = fheads * hd
    w1 = l1_fox_w_in
    wq1, wk1, wv1 = (w1[:, i * hh:(i + 1) * hh].astype(BF16) for i in range(3))
    wf1 = jnp.concatenate([w1[:, 3 * hh:3 * hh + fheads], jnp.zeros((d, LANES - fheads), F32)], axis=1).astype(BF16)
    wg1 = w1[:, 3 * hh + fheads:].astype(BF16)
    fbias = jnp.concatenate([l1_fox_forget_bias.astype(F32), jnp.zeros((LANES - fheads,), F32)])[None, :]

    ts = tiles.decay
    ns = seq // ts
    cdec = pl.pallas_call(
        functools.partial(_fox_decay_kernel, ts=ts),
        grid=(batch, ns),
        in_specs=[pl.BlockSpec((ts, d), lambda b, i: (b * ns + i, 0)), pl.BlockSpec((d, LANES), lambda b, i: (0, 0)),
                  pl.BlockSpec((1, LANES), lambda b, i: (0, 0))],
        out_specs=pl.BlockSpec((ts, LANES), lambda b, i: (b * ns + i, 0)),
        out_shape=jax.ShapeDtypeStruct((tokens, LANES), F32),
        scratch_shapes=[pltpu.VMEM((1, LANES), F32)],
        compiler_params=_params(("arbitrary", "arbitrary")),
        name="fox_decay",
    )(xn2, wf1, fbias)

    def fox_proj(w, gain, mode, out_width, out_dtype):
        return pl.pallas_call(
            functools.partial(_fox_proj_kernel, heads=fheads, mode=mode,
                              scale=(LOG2E / math.sqrt(hd)) if mode == "q" else 1.0),
            grid=(tokens // tm,),
            in_specs=[row(d), const(w.shape), const((1, hd)), row(LANES)],
            out_specs=row(out_width),
            out_shape=jax.ShapeDtypeStruct((tokens, out_width), out_dtype),
            compiler_params=_params(("parallel",)),
            name="fox_proj_" + mode,
        )(xn2, w, gain, cdec)

    q1 = fox_proj(wq1, _row(l1_fox_qk_gain[0]), "q", fheads * HEAD_PAD, BF16)
    k1 = fox_proj(wk1, _row(l1_fox_qk_gain[1]), "k", fheads * HEAD_PAD, BF16)
    v1 = fox_proj(wv1, _row(l1_fox_qk_gain[0]), "v", hh, BF16)
    gate1 = fox_proj(wg1, _row(l1_fox_qk_gain[0]), "gate", hh, F32)

    o1 = _attention(q1, k1, v1, batch=batch, seq=seq, heads=fheads, tk=tiles.attn_keys)

    wr1, br1 = _router_weights(l1_router_group, l1_router_group_bias, l1_router_expert, l1_router_expert_bias)
    h3, xn3, route1, cnt1 = _oproj_router(o1, gate1, h2, l1_fox_w_o.astype(BF16), _row(l1_norm_ffn), wr1, br1, tm=tiles.router)
    h4, _ = _moe(xn3, route1, cnt1, h3, l1_w_gate_up, l1_w_down, None, tiles)
    return h4.reshape(batch, seq, d)
```

```python
import functools
import math
from typing import NamedTuple

import jax
import jax.numpy as jnp
from jax import lax
from jax.experimental import pallas as pl
from jax.experimental.pallas import tpu as pltpu

F32 = jnp.float32
BF16 = jnp.bfloat16
I32 = jnp.int32

EPS = 1e-6
ROPE_THETA = 10000.0
LANES = 128
SUBLANES = 8
HEAD_PAD = 256
N_GROUPS = 8
EXPERTS_PER_GROUP = 8
N_EXPERTS = N_GROUPS * EXPERTS_PER_GROUP
MOE_ROWS = 256
NEG_BIG = -1e30
LOG2E = math.log2(math.e)
VMEM_LIMIT = 52 * 1024 * 1024


class _Tiles(NamedTuple):
    proj: int
    attn_keys: int
    router: int
    dest: int
    dispatch: int
    combine: int
    decay: int


def _tiles(tokens, seq):
    fit = lambda want: math.gcd(want, tokens)
    return _Tiles(proj=fit(512), attn_keys=math.gcd(512, seq // 2), router=fit(512), dest=fit(512),
                  dispatch=fit(1024), combine=fit(512), decay=math.gcd(512, seq))


def _params(sem, vmem=VMEM_LIMIT):
    return pltpu.CompilerParams(dimension_semantics=sem, vmem_limit_bytes=vmem)


def _rms(x, gain):
    ms = jnp.mean(x * x, axis=-1, keepdims=True)
    return x * lax.rsqrt(ms + EPS) * gain


def _lane_iota(shape):
    return lax.broadcasted_iota(I32, shape, len(shape) - 1)


def _rope_slab(x, cos, sin_signed):
    lane = _lane_iota((1, LANES))
    low = (lane % 64) < 32
    swapped = jnp.where(low, pltpu.roll(x, 96, 1), pltpu.roll(x, 32, 1))
    return x * cos + swapped * sin_signed


def _split3(x):
    a = x.astype(BF16)
    r = x - a.astype(F32)
    b = r.astype(BF16)
    c = (r - b.astype(F32)).astype(BF16)
    return a, b, c


def _mla_in_kernel(h_ref, pos_ref, gmix_ref, win_ref, gq_ref, gkv_ref, gkr_ref, freq_ref,
                   cq_ref, ckv_ref, kr_ref, cos_ref, sin_ref, *, q_lora, kv_lora):
    xn = _rms(h_ref[...], gmix_ref[...]).astype(BF16)
    z = jnp.dot(xn, win_ref[...], preferred_element_type=F32)
    cq_ref[...] = _rms(z[:, :q_lora], gq_ref[...]).astype(BF16)
    ckv_ref[...] = _rms(z[:, q_lora:q_lora + kv_lora], gkv_ref[...]).astype(BF16)
    kr = _rms(z[:, q_lora + kv_lora:], gkr_ref[...])
    slab = jnp.concatenate([kr, kr], axis=-1)
    ang = pos_ref[...].astype(F32) * freq_ref[...]
    lane = _lane_iota((1, LANES))
    cos = jnp.cos(ang)
    sin_signed = jnp.where((lane % 64) < 32, -jnp.sin(ang), jnp.sin(ang))
    cos_ref[...] = cos
    sin_ref[...] = sin_signed
    kr_ref[...] = _rope_slab(slab, cos, sin_signed).astype(BF16)


def _mla_q_kernel(cq_ref, w_ref, gn_ref, gr_ref, cos_ref, sin_ref, q_ref, *, heads, scale):
    q = jnp.dot(cq_ref[...], w_ref[...], preferred_element_type=F32)
    lane = _lane_iota((1, LANES))
    low = lane < 64
    cos = cos_ref[...]
    sin_signed = sin_ref[...]
    for h in range(heads):
        n = _rms(q[:, h * LANES:(h + 1) * LANES], gn_ref[...]) * scale
        q_ref[:, h * HEAD_PAD:h * HEAD_PAD + LANES] = n.astype(BF16)
    base = heads * LANES
    for p in range(heads // 2):
        slab = q[:, base + p * LANES:base + (p + 1) * LANES]
        sq = slab * slab
        ss0 = jnp.sum(jnp.where(low, sq, 0.0), axis=-1, keepdims=True)
        ss1 = jnp.sum(jnp.where(low, 0.0, sq), axis=-1, keepdims=True)
        inv = jnp.where(low, lax.rsqrt(ss0 / 64.0 + EPS), lax.rsqrt(ss1 / 64.0 + EPS))
        r = _rope_slab(slab * inv * gr_ref[...], cos, sin_signed) * scale
        zero = jnp.zeros_like(r)
        q_ref[:, (2 * p) * HEAD_PAD + LANES:(2 * p + 1) * HEAD_PAD] = jnp.where(low, r, zero).astype(BF16)
        q_ref[:, (2 * p + 1) * HEAD_PAD + LANES:(2 * p + 2) * HEAD_PAD] = jnp.where(low, zero, r).astype(BF16)


def _mla_kv_kernel(ckv_ref, w_ref, gn_ref, kr_ref, k_ref, v_ref, *, heads):
    kv = jnp.dot(ckv_ref[...], w_ref[...], preferred_element_type=F32)
    lane = _lane_iota((1, LANES))
    low = lane < 64
    kr = kr_ref[...]
    zero = jnp.zeros_like(kr)
    kr_even = jnp.where(low, kr, zero)
    kr_odd = jnp.where(low, zero, kr)
    for h in range(heads):
        n = _rms(kv[:, h * LANES:(h + 1) * LANES], gn_ref[...])
        k_ref[:, h * HEAD_PAD:h * HEAD_PAD + LANES] = n.astype(BF16)
        k_ref[:, h * HEAD_PAD + LANES:(h + 1) * HEAD_PAD] = kr_even if h % 2 == 0 else kr_odd
    v_ref[...] = kv[:, heads * LANES:].astype(BF16)


def _fox_decay_kernel(f_ref, b_ref, c_ref, carry_ref, *, ts):
    @pl.when(pl.program_id(1) == 0)
    def _():
        carry_ref[...] = jnp.zeros_like(carry_ref)

    f = f_ref[...] + b_ref[...]
    log_f = jnp.minimum(f, 0.0) - jnp.log1p(jnp.exp(-jnp.abs(f)))
    row = lax.broadcasted_iota(I32, (ts, ts), 0)
    col = lax.broadcasted_iota(I32, (ts, ts), 1)
    tri = jnp.where(col <= row, 1.0, 0.0).astype(BF16)
    a, b, c = _split3(log_f)
    parts = jnp.dot(tri, jnp.concatenate([a, b, c], axis=-1), preferred_element_type=F32)
    local = parts[:, :LANES] + parts[:, LANES:2 * LANES] + parts[:, 2 * LANES:]
    out = carry_ref[...] + local
    c_ref[...] = out
    carry_ref[...] = out[ts - 1:ts, :]


def _fox_gate_kernel(xn_ref, w_ref, gate_ref, f_ref, *, heads):
    z = jnp.dot(xn_ref[...], w_ref[...], preferred_element_type=F32)
    f_ref[...] = z[:, :LANES]
    gate_ref[...] = z[:, heads:]


def _fox_proj_kernel(xn_ref, w_ref, g_ref, c_ref, o_ref, *, heads, mode, scale):
    z = jnp.dot(xn_ref[...], w_ref[...], preferred_element_type=F32)
    if mode == "v":
        o_ref[...] = z.astype(BF16)
        return
    lane = _lane_iota((1, LANES))
    cdec = c_ref[...] * LOG2E
    for h in range(heads):
        n = _rms(z[:, h * LANES:(h + 1) * LANES], g_ref[...]) * scale
        o_ref[:, h * HEAD_PAD:h * HEAD_PAD + LANES] = n.astype(BF16)
        ch = cdec[:, h:h + 1]
        if mode == "k":
            ch = -ch
        a, b, c = (t.astype(F32) for t in _split3(ch))
        if mode == "q":
            aug = jnp.where(lane == 0, a, jnp.where(lane == 1, b, jnp.where(lane == 2, c,
                  jnp.where(lane < 6, 1.0, 0.0))))
        else:
            aug = jnp.where(lane < 3, 1.0, jnp.where(lane == 3, a, jnp.where(lane == 4, b,
                  jnp.where(lane == 5, c, 0.0))))
        o_ref[:, h * HEAD_PAD + LANES:(h + 1) * HEAD_PAD] = aug.astype(BF16)


def _attn_tile(q, k_ref, v_ref, j, m_ref, l_ref, acc_ref, *, tk, masked):
    rows = q.shape[0]
    start = pl.multiple_of(j * tk, tk)
    k = k_ref[pl.ds(start, tk), :]
    v = v_ref[pl.ds(start, tk), :]
    s = lax.dot_general(q, k, (((1,), (1,)), ((), ())), preferred_element_type=F32)
    if masked:
        row = lax.broadcasted_iota(I32, (rows, tk), 0)
        col = lax.broadcasted_iota(I32, (rows, tk), 1)
        s = jnp.where(row >= col, s, NEG_BIG)
    m_prev = m_ref[...]
    m_new = jnp.maximum(m_prev, jnp.max(s, axis=-1, keepdims=True))
    alpha = jnp.exp2(m_prev - m_new)
    p = jnp.exp2(s - jnp.concatenate([m_new] * (tk // LANES), axis=1))
    psum = p[:, :LANES]
    for c in range(1, tk // LANES):
        psum = psum + p[:, c * LANES:(c + 1) * LANES]
    l_ref[...] = alpha * l_ref[...] + psum
    acc_ref[...] = alpha * acc_ref[...] + jnp.dot(p.astype(BF16), v, preferred_element_type=F32)
    m_ref[...] = m_new


def _attn_kernel(q_ref, k_ref, v_ref, o_ref, m_a, l_a, acc_a, m_b, l_b, acc_b, *, tk):
    qi = pl.program_id(2)
    m_a[...] = jnp.full_like(m_a, NEG_BIG)
    m_b[...] = jnp.full_like(m_b, NEG_BIG)

    @pl.when((pl.program_id(0) == 0) & (pl.program_id(1) == 0) & (qi == 0))
    def _():
        for ref in (l_a, acc_a, l_b, acc_b):
            ref[...] = jnp.zeros_like(ref)

    q = q_ref[...]
    tile = functools.partial(_attn_tile, k_ref=k_ref, v_ref=v_ref, tk=tk)

    def pair(jj):
        tile(q, j=2 * jj, m_ref=m_a, l_ref=l_a, acc_ref=acc_a, masked=False)
        tile(q, j=2 * jj + 1, m_ref=m_b, l_ref=l_b, acc_ref=acc_b, masked=False)

    def body(u, carry):
        pair(2 * u)
        pair(2 * u + 1)
        return carry

    lax.fori_loop(0, qi // 2, body, 0)

    @pl.when(qi % 2 == 1)
    def _():
        pair(qi - 1)

    tile(q, j=2 * qi, m_ref=m_a, l_ref=l_a, acc_ref=acc_a, masked=True)
    low = pl.ds(tk, tk)
    tile(q_ref[low, :], j=2 * qi + 1, m_ref=m_b.at[low, :], l_ref=l_b.at[low, :], acc_ref=acc_b.at[low, :],
         masked=True)
    m = jnp.maximum(m_a[...], m_b[...])
    w_a = jnp.exp2(m_a[...] - m)
    w_b = jnp.exp2(m_b[...] - m)
    l = jnp.sum(w_a * l_a[...] + w_b * l_b[...], axis=-1, keepdims=True)
    o_ref[...] = ((w_a * acc_a[...] + w_b * acc_b[...]) / l).astype(BF16)


def _attention(q, k, v, *, batch, seq, heads, tk):
    tq = 2 * tk
    nq = seq // tq
    stat = pltpu.VMEM((tq, LANES), F32)
    return pl.pallas_call(
        functools.partial(_attn_kernel, tk=tk),
        grid=(batch, heads, nq),
        in_specs=[
            pl.BlockSpec((tq, HEAD_PAD), lambda b, h, i: (b * nq + i, h)),
            pl.BlockSpec((seq, HEAD_PAD), lambda b, h, i: (b, h)),
            pl.BlockSpec((seq, LANES), lambda b, h, i: (b, h)),
        ],
        out_specs=pl.BlockSpec((tq, LANES), lambda b, h, i: (b * nq + i, h)),
        out_shape=jax.ShapeDtypeStruct((batch * seq, heads * LANES), BF16),
        scratch_shapes=[stat] * 6,
        compiler_params=_params(("arbitrary", "arbitrary", "arbitrary")),
        name="causal_attention",
    )(q, k, v)


def _oproj_router_kernel(*refs, tm, gated):
    if gated:
        (o_ref, gate_ref, h_ref, wo_ref, gffn_ref, wr_ref, br_ref,
         h1_ref, xn_ref, route_ref, cnt_ref, carry_ref, wsplit_ref) = refs
    else:
        (o_ref, h_ref, wo_ref, gffn_ref, wr_ref, br_ref,
         h1_ref, xn_ref, route_ref, cnt_ref, carry_ref, wsplit_ref) = refs

    @pl.when(pl.program_id(0) == 0)
    def _():
        carry_ref[...] = jnp.zeros_like(carry_ref)
        w = wr_ref[...]
        w_hi = w.astype(BF16)
        wsplit_ref[:, :LANES] = w_hi
        wsplit_ref[:, LANES:] = (w - w_hi.astype(F32)).astype(BF16)

    o = o_ref[...]
    if gated:
        o = (o.astype(F32) * jax.nn.sigmoid(gate_ref[...])).astype(BF16)
    h1 = h_ref[...] + jnp.dot(o, wo_ref[...], preferred_element_type=F32)
    h1_ref[...] = h1
    xn = _rms(h1, gffn_ref[...])
    xn_ref[...] = xn

    x_hi = xn.astype(BF16)
    x_lo = (xn - x_hi.astype(F32)).astype(BF16)
    part = jnp.dot(x_hi, wsplit_ref[...], preferred_element_type=F32)
    logits = (part[:, :LANES] + part[:, LANES:]
              + jnp.dot(x_lo, wsplit_ref[:, :LANES], preferred_element_type=F32)) + br_ref[...]

    lane_i = _lane_iota((tm, LANES))
    lane = lane_i.astype(F32)
    big = 1e6
    is_g = (lane_i >= N_EXPERTS) & (lane_i < N_EXPERTS + N_GROUPS)
    gl = jnp.where(is_g, logits, NEG_BIG)
    gmax = jnp.max(gl, axis=-1, keepdims=True)
    grp = jnp.min(jnp.where(gl == gmax, lane, big), axis=-1, keepdims=True) - N_EXPERTS
    p_g = 1.0 / jnp.sum(jnp.where(is_g, jnp.exp(gl - gmax), 0.0), axis=-1, keepdims=True)

    in_grp = (lane_i // EXPERTS_PER_GROUP).astype(F32) == grp
    el = jnp.where(in_grp, logits, NEG_BIG)
    emax = jnp.max(el, axis=-1, keepdims=True)
    pe = jnp.where(in_grp, jnp.exp(el - emax), 0.0)
    pe = pe / jnp.sum(pe, axis=-1, keepdims=True)
    pm = jnp.where(in_grp, pe, -1.0)
    p1 = jnp.max(pm, axis=-1, keepdims=True)
    i1 = jnp.min(jnp.where(pm == p1, lane, big), axis=-1, keepdims=True)
    pm2 = jnp.where(lane == i1, -1.0, pm)
    p2 = jnp.max(pm2, axis=-1, keepdims=True)
    i2 = jnp.min(jnp.where(pm2 == p2, lane, big), axis=-1, keepdims=True)
    den = p1 + p2
    g1 = p_g * p1 / den
    g2 = p_g * p2 / den

    hit1 = lane == i1
    hit2 = lane == i2
    oh1 = jnp.where(hit1, 1.0, 0.0)
    oh2 = jnp.where(hit2, 1.0, 0.0)
    row = lax.broadcasted_iota(I32, (tm, tm), 0)
    col = lax.broadcasted_iota(I32, (tm, tm), 1)
    tri = jnp.where(col < row, 1.0, 0.0).astype(BF16)
    cum1 = jnp.dot(tri, oh1.astype(BF16), preferred_element_type=F32)
    cum2 = jnp.dot(tri, oh2.astype(BF16), preferred_element_type=F32)
    tot1 = jnp.sum(oh1, axis=0, keepdims=True)
    tot2 = jnp.sum(oh2, axis=0, keepdims=True)
    carry = carry_ref[...]
    r1 = jnp.sum(jnp.where(hit1, carry + cum1, 0.0), axis=-1, keepdims=True)
    r2 = jnp.sum(jnp.where(hit2, carry + tot1 + cum2, 0.0), axis=-1, keepdims=True)
    new_carry = carry + tot1 + tot2
    carry_ref[...] = new_carry
    cnt_ref[...] = new_carry

    zero = jnp.zeros((tm, LANES), F32)
    route = jnp.where(lane_i == 0, i1, jnp.where(lane_i == 1, i2,
            jnp.where(lane_i == 2, r1, jnp.where(lane_i == 3, r2,
            jnp.where(lane_i == 4, g1, jnp.where(lane_i == 5, g2, zero))))))
    route_ref[...] = route


def _oproj_router(o, gate, h, wo, gffn, wr, br, *, tm):
    tokens, d = h.shape
    gated = gate is not None
    row_spec = pl.BlockSpec((tm, d), lambda i: (i, 0))
    const = lambda shape: pl.BlockSpec(shape, lambda i: (0, 0))
    in_specs = [row_spec] + ([row_spec] if gated else []) + [
        row_spec, const(wo.shape), const((1, d)), const(wr.shape), const((1, LANES))]
    args = [o] + ([gate] if gated else []) + [h, wo, gffn, wr, br]
    return pl.pallas_call(
        functools.partial(_oproj_router_kernel, tm=tm, gated=gated),
        grid=(tokens // tm,),
        in_specs=in_specs,
        out_specs=[row_spec, row_spec, pl.BlockSpec((tm, LANES), lambda i: (i, 0)), const((1, LANES))],
        out_shape=[jax.ShapeDtypeStruct((tokens, d), F32), jax.ShapeDtypeStruct((tokens, d), F32),
                   jax.ShapeDtypeStruct((tokens, LANES), F32), jax.ShapeDtypeStruct((1, LANES), F32)],
        scratch_shapes=[pltpu.VMEM((1, LANES), F32), pltpu.VMEM((d, 2 * LANES), BF16)],
        compiler_params=_params(("arbitrary",)),
        name="oproj_router",
    )(*args)


def _plan_kernel(cnt_ref, plan_ref, blk_ref, *, n_blocks):
    cnt = cnt_ref[...]
    nblk = jnp.floor((cnt + (MOE_ROWS - 1)) * (1.0 / MOE_ROWS))
    row = lax.broadcasted_iota(I32, (LANES, LANES), 0)
    col = lax.broadcasted_iota(I32, (LANES, LANES), 1)
    upper = jnp.where(row < col, 1.0, 0.0).astype(BF16)
    nb8 = jnp.broadcast_to(nblk, (8, LANES)).astype(BF16)
    bstart = jnp.dot(nb8, upper, preferred_element_type=F32)[0:1, :]
    bend = bstart + nblk
    lane = _lane_iota((1, LANES))
    total = jnp.sum(nblk, axis=-1, keepdims=True)
    sub = lax.broadcasted_iota(I32, (8, LANES), 0)
    plan = jnp.where(sub == 0, jnp.broadcast_to(bstart, (8, LANES)),
           jnp.where(sub == 1, jnp.broadcast_to(nblk, (8, LANES)),
           jnp.where(sub == 2, jnp.broadcast_to(total, (8, LANES)), 0.0)))
    plan_ref[...] = plan
    bidx = lax.broadcasted_iota(I32, (n_blocks, LANES), 0).astype(F32)
    done = jnp.where((_lane_iota((n_blocks, LANES)) < N_EXPERTS) & (bend <= bidx), 1.0, 0.0)
    last_used = jnp.max(jnp.where(nblk > 0.0, lane.astype(F32), 0.0), axis=-1, keepdims=True)
    e_of_b = jnp.minimum(jnp.sum(done, axis=-1, keepdims=True), last_used)
    blk_ref[...] = jnp.broadcast_to(e_of_b, (n_blocks, LANES))


def _dest_kernel(route_ref, plan_ref, dest_ref, *, tm):
    route = route_ref[...]
    bstart = plan_ref[0:1, :] * float(MOE_ROWS)
    lane = _lane_iota((tm, LANES)).astype(F32)
    e1 = route[:, 0:1]
    e2 = route[:, 1:2]
    d1 =jnp.sum(jnp.where(lane == e1, bstart, 0.0), axis=-1, keepdims=True) + route[:, 2:3]
    d2 = jnp.sum(jnp.where(lane == e2, bstart, 0.0), axis=-1, keepdims=True) + route[:, 3:4]
    dest_ref[...] = jnp.where(lane == 0, d1, jnp.where(lane == 1, d2, 0.0))


def _dispatch_kernel(dest_ref, bstart_ref, nblk_ref, cnt_ref, nused_ref, x_ref, buf_hbm, zeros_ref, sem_zero,
                     sem_rows, *, td, n_blocks):
    step = pl.program_id(0)

    def for_each_fill(fn):
        def per_expert(e, c):
            base = bstart_ref[e] * MOE_ROWS + cnt_ref[e]
            head = (8 - (base & 7)) & 7
            for r in range(7):
                @pl.when(r < head)
                def _(r=r):
                    fn(pltpu.make_async_copy(zeros_ref.at[pl.ds(0, 1), :], buf_hbm.at[pl.ds(base + r, 1), :],
                                             sem_zero))
            length = (bstart_ref[e] + nblk_ref[e]) * MOE_ROWS - (base + head)
            off = base + head
            for p in (128, 64, 32, 16, 8):
                @pl.when((length & p) != 0)
                def _(off=off, p=p):
                    fn(pltpu.make_async_copy(zeros_ref.at[pl.ds(0, p), :],
                                             buf_hbm.at[pl.ds(pl.multiple_of(off, 8), p), :], sem_zero))
                off = off + (length & p)
            return c

        lax.fori_loop(0, N_EXPERTS, per_expert, 0)

        def per_tail_block(blk, c):
            fn(pltpu.make_async_copy(
                zeros_ref, buf_hbm.at[pl.ds(pl.multiple_of(blk * MOE_ROWS, MOE_ROWS), MOE_ROWS), :], sem_zero))
            return c

        lax.fori_loop(nused_ref[0], n_blocks, per_tail_block, 0)

    @pl.when(step == 0)
    def _():
        zeros_ref[...] = jnp.zeros_like(zeros_ref)
        for_each_fill(lambda copy: copy.start())

    def start_group(i, c):
        for r in range(SUBLANES):
            for k in range(2):
                pltpu.make_async_copy(x_ref.at[i, pl.ds(r, 1), :],
                                      buf_hbm.at[pl.ds(dest_ref[2 * (SUBLANES * i + r) + k], 1), :], sem_rows).start()
        return c

    lax.fori_loop(0, td // SUBLANES, start_group, 0)
    for _ in range(2):
        pltpu.make_async_copy(buf_hbm.at[pl.ds(0, td), :], buf_hbm.at[pl.ds(0, td), :], sem_rows).wait()

    @pl.when(step == pl.num_programs(0) - 1)
    def _():
        for_each_fill(lambda copy: copy.wait())


def _expert_kernel(blk_ref, nused_ref, bstart_ref, nblk_ref, x_ref, wgu_hbm, wdn_hbm, y_ref,
                   wgu_f32, wdn_f32, wgu_bf, wdn_bf, slot_ref, sems, *, d_expert):
    b = pl.program_id(0)

    def weight_copies(e, slot):
        return (pltpu.make_async_copy(wgu_hbm.at[e], wgu_f32.at[slot], sems.at[0, slot]),
                pltpu.make_async_copy(wdn_hbm.at[e], wdn_f32.at[slot], sems.at[1, slot]))

    @pl.when(b == 0)
    def _():
        slot_ref[0] = 0
        for c in weight_copies(blk_ref[0], 0):
            c.start(priority=1)

    @pl.when(b < nused_ref[0])
    def _():
        e = blk_ref[b]
        first = (b == 0) | (e != blk_ref[jnp.maximum(b - 1, 0)])

        @pl.when(first)
        def _():
            slot = slot_ref[0]
            nxt = bstart_ref[e] + nblk_ref[e]

            @pl.when(nxt < nused_ref[0])
            def _():
                for c in weight_copies(blk_ref[nxt], 1 - slot):
                    c.start(priority=1)

            for c in weight_copies(e, slot):
                c.wait()
            wgu_bf[...] = wgu_f32[slot].astype(BF16)
            wdn_bf[...] = wdn_f32[slot].astype(BF16)
            slot_ref[0] = 1 - slot

        x = x_ref[...].astype(BF16)
        gu = jnp.dot(x, wgu_bf[...], preferred_element_type=F32)
        g = gu[:, :d_expert]
        act = (g * jax.nn.sigmoid(g)) * gu[:, d_expert:]
        y_ref[...] = jnp.dot(act.astype(BF16), wdn_bf[...], preferred_element_type=F32)

    @pl.when(b >= nused_ref[0])
    def _():
        y_ref[...] = jnp.zeros_like(y_ref)


def _combine_kernel(dest_ref, dest_next_ref, y_hbm, route_ref, h_ref, g_ref, out_ref, xn_ref, rows_ref, sems,
                    *, tc, with_norm):
    step = pl.program_id(0)
    nsteps = pl.num_programs(0)
    slot = step % 2

    def issue(idx_ref, to_slot):
        def start_group(i, c):
            for r in range(SUBLANES):
                for k in range(2):
                    pltpu.make_async_copy(y_hbm.at[pl.ds(idx_ref[2 * (SUBLANES * i + r) + k], 1), :],
                                          rows_ref.at[to_slot, k, i, pl.ds(r, 1), :], sems.at[to_slot]).start()
            return c
        lax.fori_loop(0, tc // SUBLANES, start_group, 0)

    @pl.when(step == 0)
    def _():
        issue(dest_ref, 0)

    for to_slot in range(2):
        @pl.when((step + 1 < nsteps) & (slot == 1 - to_slot))
        def _(to_slot=to_slot):
            issue(dest_next_ref, to_slot)

    for k in range(2):
        pltpu.make_async_copy(y_hbm.at[pl.ds(0, tc), :], y_hbm.at[pl.ds(0, tc), :], sems.at[slot]).wait()
    route = route_ref[...]
    d = h_ref.shape[1]
    y0 = rows_ref[slot, 0].reshape(tc, d)
    y1 = rows_ref[slot, 1].reshape(tc, d)
    out = h_ref[...] + (y0 * route[:, 4:5] + y1 * route[:, 5:6])
    out_ref[...] = out
    if with_norm:
        xn_ref[...] = _rms(out, g_ref[...]).astype(BF16)
    else:
        xn_ref[...] = jnp.zeros_like(xn_ref)


def _moe(xn, route, cnt, h1, w_gate_up, w_down, next_gain, tiles):
    tokens, d = xn.shape
    d_expert = w_down.shape[1]
    n_blocks = (2 * tokens + N_EXPERTS * (MOE_ROWS - 1) + MOE_ROWS - 1) // MOE_ROWS
    assert n_blocks <= 256, "block counts must stay exact in bf16"
    nb_pad = (n_blocks + 7) // 8 * 8

    plan, blk = pl.pallas_call(
        functools.partial(_plan_kernel, n_blocks=nb_pad),
        out_shape=[jax.ShapeDtypeStruct((8, LANES), F32), jax.ShapeDtypeStruct((nb_pad, LANES), F32)],
        name="moe_plan",
    )(cnt)

    tm = tiles.dest
    dest = pl.pallas_call(
        functools.partial(_dest_kernel, tm=tm),
        grid=(tokens // tm,),
        in_specs=[pl.BlockSpec((tm, LANES), lambda i: (i, 0)), pl.BlockSpec((8, LANES), lambda i: (0, 0))],
        out_specs=pl.BlockSpec((tm, LANES), lambda i: (i, 0)),
        out_shape=jax.ShapeDtypeStruct((tokens, LANES), F32),
        compiler_params=_params(("parallel",)),
        name="moe_dest",
    )(route, plan)

    dest_flat = dest[:, :2].astype(I32).reshape(-1)
    bstart = plan[0, :N_EXPERTS].astype(I32)
    nblk = plan[1, :N_EXPERTS].astype(I32)
    n_used = plan[2, :1].astype(I32)
    blk_e = blk[:n_blocks, 0].astype(I32)

    td = tiles.dispatch
    smem = pltpu.SMEM
    buf = pl.pallas_call(
        functools.partial(_dispatch_kernel, td=td, n_blocks=n_blocks),
        grid=(tokens // td,),
        in_specs=[
            pl.BlockSpec((2 * td,), lambda i: (i,), memory_space=smem),
            pl.BlockSpec((N_EXPERTS,), lambda i: (0,), memory_space=smem),
            pl.BlockSpec((N_EXPERTS,), lambda i: (0,), memory_space=smem),
            pl.BlockSpec((N_EXPERTS,), lambda i: (0,), memory_space=smem),
            pl.BlockSpec((1,), lambda i: (0,), memory_space=smem),
            pl.BlockSpec((td // SUBLANES, SUBLANES, d), lambda i: (i, 0, 0)),
        ],
        out_specs=pl.BlockSpec(memory_space=pl.ANY),
        out_shape=jax.ShapeDtypeStruct((n_blocks * MOE_ROWS, d), F32),
        scratch_shapes=[pltpu.VMEM((MOE_ROWS, d), F32), pltpu.SemaphoreType.DMA(()), pltpu.SemaphoreType.DMA(())],
        compiler_params=_params(("arbitrary",)),
        name="moe_dispatch",
    )(dest_flat, bstart, nblk, cnt[0, :N_EXPERTS].astype(I32), n_used, xn.reshape(tokens // SUBLANES, SUBLANES, d))

    y = pl.pallas_call(
        functools.partial(_expert_kernel, d_expert=d_expert),
        grid_spec=pltpu.PrefetchScalarGridSpec(
            num_scalar_prefetch=4,
            grid=(n_blocks,),
            in_specs=[
                pl.BlockSpec((MOE_ROWS, d), lambda b, *_: (b, 0)),
                pl.BlockSpec(memory_space=pl.ANY),
                pl.BlockSpec(memory_space=pl.ANY),
            ],
            out_specs=pl.BlockSpec((MOE_ROWS, d), lambda b, *_: (b, 0)),
            scratch_shapes=[pltpu.VMEM((2, d, 2 * d_expert), F32), pltpu.VMEM((2, d_expert, d), F32),
                            pltpu.VMEM((d, 2 * d_expert), BF16), pltpu.VMEM((d_expert, d), BF16),
                            pltpu.SMEM((1,), I32), pltpu.SemaphoreType.DMA((2, 2))],
        ),
        out_shape=jax.ShapeDtypeStruct((n_blocks * MOE_ROWS, d), F32),
        compiler_params=_params(("arbitrary",)),
        name="moe_experts",
    )(blk_e, n_used, bstart, nblk, buf, w_gate_up, w_down)

    tc = tiles.combine
    with_norm = next_gain is not None
    gain = next_gain if with_norm else jnp.ones((1, d), F32)
    row_spec = pl.BlockSpec((tc, d), lambda i: (i, 0))
    out, xn_next = pl.pallas_call(
        functools.partial(_combine_kernel, tc=tc, with_norm=with_norm),
        grid=(tokens // tc,),
        in_specs=[
            pl.BlockSpec((2 * tc,), lambda i: (i,), memory_space=smem),
            pl.BlockSpec((2 * tc,), lambda i: (jnp.minimum(i + 1, tokens // tc - 1),), memory_space=smem),
            pl.BlockSpec(memory_space=pl.ANY),
            pl.BlockSpec((tc, LANES), lambda i: (i, 0)),
            row_spec,
            pl.BlockSpec((1, d), lambda i: (0, 0)),
        ],
        out_specs=[row_spec, row_spec if with_norm else pl.BlockSpec((8, LANES), lambda i: (0, 0))],
        out_shape=[jax.ShapeDtypeStruct((tokens, d), F32),
                   jax.ShapeDtypeStruct((tokens, d), BF16) if with_norm else jax.ShapeDtypeStruct((8, LANES), BF16)],
        scratch_shapes=[pltpu.VMEM((2, 2, tc // SUBLANES, SUBLANES, d), F32), pltpu.SemaphoreType.DMA((2,))],
        compiler_params=_params(("arbitrary",)),
        name="moe_combine",
    )(dest_flat, dest_flat, y, route, h1, gain)
    return out, (xn_next if with_norm else None)


def _router_weights(w_group, b_group, w_expert, b_expert):
    d = w_group.shape[0]
    pad = LANES - N_EXPERTS - N_GROUPS
    w = jnp.concatenate([w_expert, w_group, jnp.zeros((d, pad), F32)], axis=1)
    b = jnp.concatenate([b_expert, b_group, jnp.zeros((pad,), F32)])[None, :]
    return w, b


def _row(v):
    return v.astype(F32)[None, :]


def kernel(x, positions, l0_norm_mix, l0_mla_w_in, l0_mla_q_lat_norm, l0_mla_kv_lat_norm, l0_mla_w_uq, l0_mla_w_ukv, l0_mla_qk_gain, l0_mla_w_o, l0_norm_ffn, l0_router_group, l0_router_group_bias, l0_router_expert, l0_router_expert_bias, l0_w_gate_up, l0_w_down, l1_norm_mix, l1_fox_w_in, l1_fox_forget_bias, l1_fox_qk_gain, l1_fox_w_o, l1_norm_ffn, l1_router_group, l1_router_group_bias, l1_router_expert, l1_router_expert_bias, l1_w_gate_up, l1_w_down):
    batch, seq, d = x.shape
    tokens = batch * seq
    q_lora = l0_mla_q_lat_norm.shape[0]
    kv_lora = l0_mla_kv_lat_norm.shape[0]
    rope = l0_mla_w_in.shape[1] - q_lora - kv_lora
    nope = l0_mla_qk_gain.shape[1] - rope
    heads = l0_mla_w_uq.shape[1] // (nope + rope)
    assert rope == 64 and nope == LANES and heads % 2 == 0
    tiles = _tiles(tokens, seq)

    h0 = x.reshape(tokens, d)
    pos = positions.reshape(tokens, 1)

    half = rope // 2
    freqs = ROPE_THETA ** (-jnp.arange(half, dtype=F32) / half)
    freq_row = jnp.tile(freqs, LANES // half)[None, :]
    gkr = _row(l0_mla_qk_gain[1, nope:])
    tm = tiles.proj
    row = lambda w: pl.BlockSpec((tm, w), lambda i: (i, 0))
    const = lambda shape: pl.BlockSpec(shape, lambda i: (0, 0))
    win = l0_mla_w_in.astype(BF16)
    cq, ckv, kr2, cos_t, sin_t = pl.pallas_call(
        functools.partial(_mla_in_kernel, q_lora=q_lora, kv_lora=kv_lora),
        grid=(tokens // tm,),
        in_specs=[row(d), row(1), const((1, d)), const(win.shape), const((1, q_lora)), const((1, kv_lora)),
                  const((1, rope)), const((1, LANES))],
        out_specs=[row(q_lora), row(kv_lora), row(LANES), row(LANES), row(LANES)],
        out_shape=[jax.ShapeDtypeStruct((tokens, q_lora), BF16), jax.ShapeDtypeStruct((tokens, kv_lora), BF16),
                   jax.ShapeDtypeStruct((tokens, LANES), BF16), jax.ShapeDtypeStruct((tokens, LANES), F32),
                   jax.ShapeDtypeStruct((tokens, LANES), F32)],
        compiler_params=_params(("parallel",)),
        name="mla_in",
    )(h0, pos, _row(l0_norm_mix), win, _row(l0_mla_q_lat_norm), _row(l0_mla_kv_lat_norm), gkr, freq_row)

    wuq = l0_mla_w_uq.reshape(q_lora, heads, nope + rope)
    wuq = jnp.concatenate([wuq[:, :, :nope].reshape(q_lora, heads * nope),
                           wuq[:, :, nope:].reshape(q_lora, heads * rope)], axis=1).astype(BF16)
    scale = LOG2E / math.sqrt(nope + rope)
    q_aug = pl.pallas_call(
        functools.partial(_mla_q_kernel, heads=heads, scale=scale),
        grid=(tokens // tm,),
        in_specs=[row(q_lora), const(wuq.shape), const((1, nope)), const((1, LANES)), row(LANES), row(LANES)],
        out_specs=row(heads * HEAD_PAD),
        out_shape=jax.ShapeDtypeStruct((tokens, heads * HEAD_PAD), BF16),
        compiler_params=_params(("parallel",)),
        name="mla_q",
    )(cq, wuq, _row(l0_mla_qk_gain[0, :nope]), _row(jnp.tile(l0_mla_qk_gain[0, nope:], 2)), cos_t, sin_t)

    v_dim = l0_mla_w_ukv.shape[1] // heads - nope
    assert v_dim == LANES
    wukv = l0_mla_w_ukv.reshape(kv_lora, heads, nope + v_dim)
    wukv = jnp.concatenate([wukv[:, :, :nope].reshape(kv_lora, heads * nope),
                            wukv[:, :, nope:].reshape(kv_lora, heads * v_dim)], axis=1).astype(BF16)
    k_aug, v0 = pl.pallas_call(
        functools.partial(_mla_kv_kernel, heads=heads),
        grid=(tokens // tm,),
        in_specs=[row(kv_lora), const(wukv.shape), const((1, nope)), row(LANES)],
        out_specs=[row(heads * HEAD_PAD), row(heads * v_dim)],
        out_shape=[jax.ShapeDtypeStruct((tokens, heads * HEAD_PAD), BF16),
                   jax.ShapeDtypeStruct((tokens, heads * v_dim), BF16)],
        compiler_params=_params(("parallel",)),
        name="mla_kv",
    )(ckv, wukv, _row(l0_mla_qk_gain[1, :nope]), kr2)

    o0 = _attention(q_aug, k_aug, v0, batch=batch, seq=seq, heads=heads, tk=tiles.attn_keys)

    wr0, br0 = _router_weights(l0_router_group, l0_router_group_bias, l0_router_expert, l0_router_expert_bias)
    h1, xn1, route0, cnt0 = _oproj_router(o0, None, h0, l0_mla_w_o.astype(BF16), _row(l0_norm_ffn), wr0, br0, tm=tiles.router)
    h2, xn2 = _moe(xn1, route0, cnt0, h1, l0_w_gate_up, l0_w_down, _row(l1_norm_mix), tiles)

    fheads = l1_fox_forget_bias.shape[0]
    hd = l1_fox_qk_gain.shape[1]
    assert hd == LANES
    hh = fheads * hd
    w1 = l1_fox_w_in
    wq1, wk1, wv1 = (w1[:, i * hh:(i + 1) * hh].astype(BF16) for i in range(3))
    wfg1 = w1[:, 3 * hh:].astype(BF16)
    fbias = jnp.concatenate([l1_fox_forget_bias.astype(F32), jnp.zeros((LANES - fheads,), F32)])[None, :]

    gate1, f_logit = pl.pallas_call(
        functools.partial(_fox_gate_kernel, heads=fheads),
        grid=(tokens // tm,),
        in_specs=[row(d), const(wfg1.shape)],
        out_specs=[row(hh), row(LANES)],
        out_shape=[jax.ShapeDtypeStruct((tokens, hh), F32), jax.ShapeDtypeStruct((tokens, LANES), F32)],
        compiler_params=_params(("parallel",)),
        name="fox_proj_gate",
    )(xn2, wfg1)

    ts = tiles.decay
    ns = seq // ts
    cdec = pl.pallas_call(
        functools.partial(_fox_decay_kernel, ts=ts),
        grid=(batch, ns),
        in_specs=[pl.BlockSpec((ts, LANES), lambda b, i: (b * ns + i, 0)), pl.BlockSpec((1, LANES), lambda b, i: (0, 0))],
        out_specs=pl.BlockSpec((ts, LANES), lambda b, i: (b * ns + i, 0)),
        out_shape=jax.ShapeDtypeStruct((tokens, LANES), F32),
        scratch_shapes=[pltpu.VMEM((1, LANES), F32)],
        compiler_params=_params(("arbitrary", "arbitrary")),
        name="fox_decay",
    )(f_logit, fbias)

    def fox_proj(w, gain, mode, out_width, out_dtype):
        return pl.pallas_call(
            functools.partial(_fox_proj_kernel, heads=fheads, mode=mode,
                              scale=(LOG2E / math.sqrt(hd)) if mode == "q" else 1.0),
            grid=(tokens // tm,),
            in_specs=[row(d), const(w.shape), const((1, hd)), row(LANES)],
            out_specs=row(out_width),
            out_shape=jax.ShapeDtypeStruct((tokens, out_width), out_dtype),
            compiler_params=_params(("parallel",)),
            name="fox_proj_" + mode,
        )(xn2, w, gain, cdec)

    q1 = fox_proj(wq1, _row(l1_fox_qk_gain[0]), "q", fheads * HEAD_PAD, BF16)
    k1 = fox_proj(wk1, _row(l1_fox_qk_gain[1]), "k", fheads * HEAD_PAD, BF16)
    v1 = fox_proj(wv1, _row(l1_fox_qk_gain[0]), "v", hh, BF16)

    o1 = _attention(q1, k1, v1, batch=batch, seq=seq, heads=fheads, tk=tiles.attn_keys)

    wr1, br1 = _router_weights(l1_router_group, l1_router_group_bias, l1_router_expert, l1_router_expert_bias)
    h3, xn3, route1, cnt1 = _oproj_router(o1, gate1, h2, l1_fox_w_o.astype(BF16), _row(l1_norm_ffn), wr1, br1, tm=tiles.router)
    h4, _ = _moe(xn3, route1, cnt1, h3, l1_w_gate_up, l1_w_down, None, tiles)
    return h4.reshape(batch, seq, d)
```
